```python
import math, functools
import jax, jax.numpy as jnp
from jax import lax
import numpy as np

D_MODEL = 1024
BATCH = 4
SEQ = 4096
DEPTH = 2
DEC_BATCH = 32
DEC_SEQ = 8
PAST_LEN = 16384
PAGE_SIZE = 128

D_CONV = D_MODEL // 2
CONV_WIDTH = 31
RET_HEADS = 4
RET_DK = D_MODEL // 16
RET_DV = D_MODEL // 8
RET_CHUNK = 128
ROPE_BASE = 10000.0
FOX_HEADS = 8
FOX_HD = D_MODEL // 16
FOX_QBLOCK = 128
FORGET_BIAS = 3.0
N_MEM = 256
X_HEADS = 4
X_HD = D_MODEL // 8
D_FF = 4 * D_MODEL
N_BRANCH = 3
EPS = 1e-6
SPLIT_SIZES = (D_CONV, D_CONV, RET_HEADS * RET_DK, RET_HEADS * RET_DK, RET_HEADS * RET_DV, RET_HEADS * RET_DV, FOX_HEADS * FOX_HD, FOX_HEADS * FOX_HD, FOX_HEADS * FOX_HD, FOX_HEADS, N_BRANCH * D_MODEL)
N_IN = sum(SPLIT_SIZES)
FORGET_COL = sum(SPLIT_SIZES[:9])

kernel_name = 'hybrid_conv_retention_fox_step'


def rms_norm(x, g):
    xf = x.astype(jnp.float32)
    y = xf * lax.rsqrt(jnp.mean(jnp.square(xf), axis=-1, keepdims=True) + EPS)
    return (y * g.astype(jnp.float32)).astype(x.dtype)


def layer_norm(x, g, b):
    xf = x.astype(jnp.float32)
    xc = xf - jnp.mean(xf, axis=-1, keepdims=True)
    y = xc * lax.rsqrt(jnp.mean(jnp.square(xc), axis=-1, keepdims=True) + EPS)
    return (y * g.astype(jnp.float32) + b.astype(jnp.float32)).astype(x.dtype)


def split_in(z):
    parts, start = [], 0
    for size in SPLIT_SIZES:
        parts.append(z[..., start:start + size])
        start += size
    return parts


def rotary(x, pos):
    half = x.shape[-1] // 2
    inv_freq = jnp.exp(-math.log(ROPE_BASE) * jnp.arange(half, dtype=jnp.float32) / half)
    ang = pos.astype(jnp.float32)[:, None] * inv_freq[None, :]
    cos = jnp.cos(ang)[None, :, None, :]
    sin = jnp.sin(ang)[None, :, None, :]
    xf = x.astype(jnp.float32)
    x1, x2 = xf[..., :half], xf[..., half:]
    return jnp.concatenate([x1 * cos - x2 * sin, x1 * sin + x2 * cos], axis=-1).astype(x.dtype)


def conv_branch(a, b, conv_buf, conv_w, conv_b, ln_g, ln_b, w_pw, b_pw):
    u = a * jax.nn.sigmoid(b)
    ext = jnp.concatenate([conv_buf.astype(u.dtype), u], axis=1)
    y = lax.conv_general_dilated(ext, conv_w[:, None, :].astype(u.dtype), window_strides=(1,), padding='VALID',
                                 dimension_numbers=('NWC', 'WIO', 'NWC'), feature_group_count=D_CONV) + conv_b
    y = jax.nn.silu(layer_norm(y, ln_g, ln_b))
    return y @ w_pw + b_pw, ext[:, -(CONV_WIDTH - 1):]


def retention(q, k, v, state0):
    B, L = q.shape[0], q.shape[1]
    C = math.gcd(L, RET_CHUNK)
    n = L // C
    log_g = jnp.log1p(-jnp.exp2(-5.0 - jnp.arange(RET_HEADS, dtype=jnp.float32)))
    idx = jnp.arange(C, dtype=jnp.float32)
    diff = idx[:, None] - idx[None, :]
    decay_mask = jnp.where(diff >= 0, jnp.exp(log_g[:, None, None] * jnp.maximum(diff, 0.0)), 0.0)
    q_dec = jnp.exp(log_g[:, None] * (idx[None, :] + 1.0))
    k_dec = jnp.exp(log_g[:, None] * (C - 1.0 - idx[None, :]))
    chunk_dec = jnp.exp(log_g * C)

    def to_chunks(t):
        return t.astype(jnp.float32).reshape(B, n, C, RET_HEADS, -1).transpose(1, 0, 3, 2, 4)

    qc = to_chunks(q) * (RET_DK ** -0.5)
    kc, vc = to_chunks(k), to_chunks(v)

    def step(S, inp):
        qi, ki, vi = inp
        inner = jnp.einsum('bhid,bhjd->bhij', qi, ki) * decay_mask
        o = jnp.einsum('bhij,bhjv->bhiv', inner, vi) + jnp.einsum('bhid,bhdv->bhiv', qi, S) * q_dec[None, :, :, None]
        S = S * chunk_dec[None, :, None, None] + jnp.einsum('bhjd,bhjv->bhdv', ki * k_dec[None, :, :, None], vi)
        return S, o

    S, o = lax.scan(step, state0.astype(jnp.float32), (qc, kc, vc))
    o = o.transpose(1, 0, 3, 2, 4).reshape(B, L, RET_HEADS, RET_DV)
    return o.astype(q.dtype), S


def fox_attend_prompt(q, k, v, logf):
    B, S = q.shape[0], q.shape[1]
    c = jnp.cumsum(logf, axis=1).transpose(0, 2, 1)
    kpos = jnp.arange(S)
    scale = FOX_HD ** -0.5

    def block(i):
        start = i * FOX_QBLOCK
        qb = lax.dynamic_slice_in_dim(q, start, FOX_QBLOCK, axis=1)
        cb = lax.dynamic_slice_in_dim(c, start, FOX_QBLOCK, axis=2)
        s = jnp.einsum('bqhd,bkhd->bhqk', qb, k, preferred_element_type=jnp.float32) * scale + cb[..., :, None] - c[:, :, None, :]
        qpos = start + jnp.arange(FOX_QBLOCK)
        s = jnp.where(kpos[None, :] <= qpos[:, None], s, -jnp.inf)
        p = jax.nn.softmax(s, axis=-1)
        return jnp.einsum('bhqk,bkhd->bqhd', p.astype(v.dtype), v)

    o = lax.map(block, jnp.arange(S // FOX_QBLOCK))
    return o.transpose(1, 0, 2, 3, 4).reshape(B, S, FOX_HEADS, FOX_HD)


def fox_attend_sample(q, k, v, logf, k_past, v_past, logf_past):
    scale = FOX_HD ** -0.5
    T, P = q.shape[1], k_past.shape[1]
    cum_past = jnp.cumsum(logf_past.astype(jnp.float32), axis=1)
    bias_past = (cum_past[:, -1:] - cum_past).transpose(0, 2, 1)
    c_new = jnp.cumsum(logf, axis=1).transpose(0, 2, 1)
    s_past = jnp.einsum('bqhd,bkhd->bhqk', q, k_past, preferred_element_type=jnp.float32) * scale + c_new[..., :, None] + bias_past[:, :, None, :]
    s_new = jnp.einsum('bqhd,bkhd->bhqk', q, k, preferred_element_type=jnp.float32) * scale + c_new[..., :, None] - c_new[..., None, :]
    causal = jnp.arange(T)[None, :] <= jnp.arange(T)[:, None]
    s_new = jnp.where(causal, s_new, -jnp.inf)
    p = jax.nn.softmax(jnp.concatenate([s_past, s_new], axis=-1), axis=-1)
    o = jnp.einsum('bhqk,bkhd->bqhd', p[..., :P].astype(v_past.dtype), v_past)
    return o + jnp.einsum('bhqk,bkhd->bqhd', p[..., P:].astype(v.dtype), v)


def memory_kv(mem, g_mem, w_k, w_v, g_k):
    B, M, _ = mem.shape
    m = rms_norm(mem, g_mem)
    k = rms_norm((m @ w_k).reshape(B, M, X_HEADS, X_HD), g_k)
    v = (m @ w_v).reshape(B, M, X_HEADS, X_HD)
    return k, v


def cross_attend(h, mem_k, mem_v, w_q, g_q, w_o):
    B, L, _ = h.shape
    q = rms_norm((h @ w_q).reshape(B, L, X_HEADS, X_HD), g_q)
    s = jnp.einsum('bqhd,bkhd->bhqk', q, mem_k, preferred_element_type=jnp.float32) * (X_HD ** -0.5)
    p = jax.nn.softmax(s, axis=-1)
    o = jnp.einsum('bhqk,bkhd->bqhd', p.astype(mem_v.dtype), mem_v)
    return o.reshape(B, L, X_HEADS * X_HD) @ w_o


def decoder_layer(x, pos, conv_buf, ret_state, fox_attend, mem_k, mem_v,
                  g_mix, w_in, b_in, conv_w, conv_b, conv_ln_g, conv_ln_b, conv_w_pw, conv_b_pw,
                  ret_gn, ret_w_o, fox_g_q, fox_g_k, fox_w_o, w_out,
                  g_x, x_w_q, x_g_q, x_w_o, g_ff, w_ff1, w_ff2):
    B, L, _ = x.shape
    h = rms_norm(x, g_mix)
    z = h @ w_in + b_in
    ca, cb, rq, rk, rv, rg, fq, fk, fv, ff, gl = split_in(z)
    y_conv, new_buf = conv_branch(ca, cb, conv_buf, conv_w, conv_b, conv_ln_g, conv_ln_b, conv_w_pw, conv_b_pw)
    rq = rotary(rq.reshape(B, L, RET_HEADS, RET_DK), pos)
    rk = rotary(rk.reshape(B, L, RET_HEADS, RET_DK), pos)
    o_ret, new_state = retention(rq, rk, rv.reshape(B, L, RET_HEADS, RET_DV), ret_state)
    o_ret = rms_norm(o_ret, ret_gn.reshape(RET_HEADS, RET_DV)).reshape(B, L, RET_HEADS * RET_DV)
    y_ret = (jax.nn.silu(rg) * o_ret) @ ret_w_o
    fq = rms_norm(fq.reshape(B, L, FOX_HEADS, FOX_HD), fox_g_q)
    fk = rms_norm(fk.reshape(B, L, FOX_HEADS, FOX_HD), fox_g_k)
    fv = fv.reshape(B, L, FOX_HEADS, FOX_HD)
    logf = jax.nn.log_sigmoid(ff.astype(jnp.float32))
    y_fox = fox_attend(fq, fk, fv, logf).reshape(B, L, FOX_HEADS * FOX_HD) @ fox_w_o
    gates = jax.nn.sigmoid(gl).reshape(B, L, N_BRANCH, D_MODEL)
    merged = gates[:, :, 0] * y_conv + gates[:, :, 1] * y_ret + gates[:, :, 2] * y_fox
    x = x + merged @ w_out
    x = x + cross_attend(rms_norm(x, g_x), mem_k, mem_v, x_w_q, x_g_q, x_w_o)
    hf = rms_norm(x, g_ff)
    x = x + jnp.square(jax.nn.relu(hf @ w_ff1)) @ w_ff2
    return x, new_buf, new_state, fk, fv, logf


def setup_inputs(seed: int = 0) -> dict:
    key = jax.random.key(seed)
    ks = list(jax.random.split(key, 48))

    def nrm(shape, scale):
        return scale * jax.random.normal(ks.pop(), shape, jnp.float32)

    def gain(shape):
        return 1.0 + nrm(shape, 0.02)

    n_pages = PAST_LEN // PAGE_SIZE
    n_used = DEC_BATCH * n_pages
    n_pool = n_used + n_used // 4
    x_prompt = nrm((BATCH, SEQ, D_MODEL), 1.0)
    x_sample = nrm((DEC_BATCH, DEC_SEQ, D_MODEL), 1.0)
    cache_conv = nrm((DEPTH, DEC_BATCH, CONV_WIDTH - 1, D_CONV), 0.5)
    state_ret = nrm((DEPTH, DEC_BATCH, RET_HEADS, RET_DK, RET_DV), 4.0)
    cache_fox_k = nrm((DEPTH, n_pool, PAGE_SIZE, FOX_HEADS, FOX_HD), 1.0)
    cache_fox_v = nrm((DEPTH, n_pool, PAGE_SIZE, FOX_HEADS, FOX_HD), 1.0)
    cache_fox_logf = jax.nn.log_sigmoid(FORGET_BIAS + nrm((DEPTH, n_pool, PAGE_SIZE, FOX_HEADS), 0.5))
    cache_mem_k = nrm((DEPTH, DEC_BATCH, N_MEM, X_HEADS, X_HD), 1.0)
    cache_mem_v = nrm((DEPTH, DEC_BATCH, N_MEM, X_HEADS, X_HD), 1.0)
    page_table = jax.random.permutation(ks.pop(), n_pool)[:n_used].reshape(DEC_BATCH, n_pages).astype(jnp.int32)
    mem_prompt = nrm((BATCH, N_MEM, D_MODEL), 1.0)
    dm = D_MODEL ** -0.5
    b_in = nrm((DEPTH, N_IN), 0.02).at[:, FORGET_COL:FORGET_COL + FOX_HEADS].add(FORGET_BIAS)
    return {
        'x_prompt': x_prompt, 'x_sample': x_sample,
        'cache_conv': cache_conv, 'state_ret': state_ret,
        'cache_fox_k': cache_fox_k, 'cache_fox_v': cache_fox_v, 'cache_fox_logf': cache_fox_logf,
        'cache_mem_k': cache_mem_k, 'cache_mem_v': cache_mem_v,
        'page_table': page_table, 'mem_prompt': mem_prompt,
        'g_mix': gain((DEPTH, D_MODEL)),
        'w_in': nrm((DEPTH, D_MODEL, N_IN), dm),
        'b_in': b_in,
        'conv_w': nrm((DEPTH, CONV_WIDTH, D_CONV), CONV_WIDTH ** -0.5),
        'conv_b': nrm((DEPTH, D_CONV), 0.02),
        'conv_ln_g': gain((DEPTH, D_CONV)),
        'conv_ln_b': nrm((DEPTH, D_CONV), 0.02),
        'conv_w_pw': nrm((DEPTH, D_CONV, D_MODEL), D_CONV ** -0.5),
        'conv_b_pw': nrm((DEPTH, D_MODEL), 0.02),
        'ret_gn': gain((DEPTH, RET_HEADS * RET_DV)),
        'ret_w_o': nrm((DEPTH, RET_HEADS * RET_DV, D_MODEL), (RET_HEADS * RET_DV) ** -0.5),
        'fox_g_q': gain((DEPTH, FOX_HD)),
        'fox_g_k': gain((DEPTH, FOX_HD)),
        'fox_w_o': nrm((DEPTH, FOX_HEADS * FOX_HD, D_MODEL), (FOX_HEADS * FOX_HD) ** -0.5),
        'w_out': nrm((DEPTH, D_MODEL, D_MODEL), dm),
        'g_x': gain((DEPTH, D_MODEL)),
        'g_mem': gain((DEPTH, D_MODEL)),
        'x_w_q': nrm((DEPTH, D_MODEL, X_HEADS * X_HD), dm),
        'x_w_k': nrm((DEPTH, D_MODEL, X_HEADS * X_HD), dm),
        'x_w_v': nrm((DEPTH, D_MODEL, X_HEADS * X_HD), dm),
        'x_g_q': gain((DEPTH, X_HD)),
        'x_g_k': gain((DEPTH, X_HD)),
        'x_w_o': nrm((DEPTH, X_HEADS * X_HD, D_MODEL), (X_HEADS * X_HD) ** -0.5),
        'g_ff': gain((DEPTH, D_MODEL)),
        'w_ff1': nrm((DEPTH, D_MODEL, D_FF), dm),
        'w_ff2': nrm((DEPTH, D_FF, D_MODEL), D_FF ** -0.5),
    }


def reference(x_prompt, x_sample, cache_conv, state_ret, cache_fox_k, cache_fox_v, cache_fox_logf,
              cache_mem_k, cache_mem_v, page_table, mem_prompt,
              g_mix, w_in, b_in, conv_w, conv_b, conv_ln_g, conv_ln_b, conv_w_pw, conv_b_pw,
              ret_gn, ret_w_o, fox_g_q, fox_g_k, fox_w_o, w_out,
              g_x, g_mem, x_w_q, x_w_k, x_w_v, x_g_q, x_g_k, x_w_o, g_ff, w_ff1, w_ff2):
    bp, sp = x_prompt.shape[0], x_prompt.shape[1]
    bs, ss = x_sample.shape[0], x_sample.shape[1]
    n_pages = page_table.shape[1]
    past = n_pages * PAGE_SIZE
    pos_p = jnp.arange(sp)
    pos_s = past + jnp.arange(ss)
    xp, xs = x_prompt, x_sample
    conv_p, ret_p, fk_p, fv_p, lf_p, mk_p, mv_p = [], [], [], [], [], [], []
    conv_s, ret_s, fk_s, fv_s, lf_s = [], [], [], [], []
    for l in range(DEPTH):
        lw = (g_mix[l], w_in[l], b_in[l], conv_w[l], conv_b[l], conv_ln_g[l], conv_ln_b[l], conv_w_pw[l], conv_b_pw[l],
              ret_gn[l], ret_w_o[l], fox_g_q[l], fox_g_k[l], fox_w_o[l], w_out[l],
              g_x[l], x_w_q[l], x_g_q[l], x_w_o[l], g_ff[l], w_ff1[l], w_ff2[l])
        mem_k, mem_v = memory_kv(mem_prompt, g_mem[l], x_w_k[l], x_w_v[l], x_g_k[l])
        conv0 = jnp.zeros((bp, CONV_WIDTH - 1, D_CONV), xp.dtype)
        ret0 = jnp.zeros((bp, RET_HEADS, RET_DK, RET_DV), jnp.float32)
        xp, c_b, r_s, f_k, f_v, l_f = decoder_layer(xp, pos_p, conv0, ret0, fox_attend_prompt, mem_k, mem_v, *lw)
        conv_p.append(c_b); ret_p.append(r_s); fk_p.append(f_k); fv_p.append(f_v); lf_p.append(l_f)
        mk_p.append(mem_k); mv_p.append(mem_v)
        k_past = cache_fox_k[l][page_table].reshape(bs, past, FOX_HEADS, FOX_HD)
        v_past = cache_fox_v[l][page_table].reshape(bs, past, FOX_HEADS, FOX_HD)
        lf_past = cache_fox_logf[l][page_table].reshape(bs, past, FOX_HEADS)
        attend_s = functools.partial(fox_attend_sample, k_past=k_past, v_past=v_past, logf_past=lf_past)
        xs, c_b, r_s, f_k, f_v, l_f = decoder_layer(xs, pos_s, cache_conv[l], state_ret[l], attend_s,
                                                   cache_mem_k[l], cache_mem_v[l], *lw)
        conv_s.append(c_b); ret_s.append(r_s); fk_s.append(f_k); fv_s.append(f_v); lf_s.append(l_f)
    new_conv_p = jnp.stack(conv_p)
    new_ret_p = jnp.stack(ret_p)
    new_fox_k_p = jnp.stack(fk_p)
    new_fox_v_p = jnp.stack(fv_p)
    new_fox_logf_p = jnp.stack(lf_p)
    new_mem_k_p = jnp.stack(mk_p)
    new_mem_v_p = jnp.stack(mv_p)
    new_conv_s = jnp.stack(conv_s)
    new_ret_s = jnp.stack(ret_s)
    new_fox_k_s = jnp.stack(fk_s)
    new_fox_v_s = jnp.stack(fv_s)
    new_fox_logf_s = jnp.stack(lf_s)
    return (xp, xs, new_conv_p, new_ret_p, new_fox_k_p, new_fox_v_p, new_fox_logf_p, new_mem_k_p, new_mem_v_p,
            new_conv_s, new_ret_s, new_fox_k_s, new_fox_v_s, new_fox_logf_s)
```

```python
import functools
import math

import jax
import jax.numpy as jnp
from jax import lax
from jax.experimental import pallas as pl
from jax.experimental.pallas import tpu as pltpu

F32 = jnp.float32
BF16 = jnp.bfloat16
EPS = 1e-6
ROPE_BASE = 10000.0
RET_CHUNK = 128
LANES = 128
SUBLANES = 8
VMEM_LIMIT = 56 * 1024 * 1024
NEG_INF = float("-inf")

_NT = (((1,), (1,)), ((), ()))


def _params(n_grid):
    return pltpu.CompilerParams(dimension_semantics=("arbitrary",) * n_grid,
                                vmem_limit_bytes=VMEM_LIMIT)


def _split2(x):
    hi = x.astype(BF16)
    lo = (x - hi.astype(F32)).astype(BF16)
    return hi, lo


def _split3(x):
    hi = x.astype(BF16)
    r = x - hi.astype(F32)
    mid = r.astype(BF16)
    lo = (r - mid.astype(F32)).astype(BF16)
    return hi, mid, lo


def _dot(a, b):
    return jnp.dot(a, b, preferred_element_type=F32)


def _rms(x, g):
    return x * lax.rsqrt(jnp.mean(x * x, axis=-1, keepdims=True) + EPS) * g


def _log_sigmoid(x):
    return jnp.minimum(x, 0.0) - jnp.log1p(jnp.exp(-jnp.abs(x)))


def _sigmoid(x):
    return 1.0 / (1.0 + jnp.exp(-x))


def _pick(n, cands):
    for c in cands:
        if n % c == 0:
            return c
    return n


def _in_proj_kernel(x_ref, g_ref, w_ref, b_ref, wf_ref, bf_ref, z_ref, lf_ref, h_scr):
    @pl.when(pl.program_id(1) == 0)
    def _():
        hb = _rms(x_ref[...], g_ref[...]).astype(BF16)
        h_scr[...] = hb
        lf_ref[...] = _log_sigmoid(_dot(hb, wf_ref[...]) + bf_ref[...])

    z_ref[...] = _dot(h_scr[...], w_ref[...]) + b_ref[...]


def _in_proj(x, g, w, b, wf, bf):
    t, d = x.shape
    n = w.shape[1]
    tm = _pick(t, (512, 256))
    tn = 1024
    return pl.pallas_call(
        _in_proj_kernel,
        grid=(t // tm, n // tn),
        in_specs=[
            pl.BlockSpec((tm, d), lambda i, j: (i, 0)),
            pl.BlockSpec((1, d), lambda i, j: (0, 0)),
            pl.BlockSpec((d, tn), lambda i, j: (0, j)),
            pl.BlockSpec((1, tn), lambda i, j: (0, j)),
            pl.BlockSpec((d, LANES), lambda i, j: (0, 0)),
            pl.BlockSpec((1, LANES), lambda i, j: (0, 0)),
        ],
        out_specs=[
            pl.BlockSpec((tm, tn), lambda i, j: (i, j)),
            pl.BlockSpec((tm, LANES), lambda i, j: (i, 0)),
        ],
        out_shape=[jax.ShapeDtypeStruct((t, n), F32), jax.ShapeDtypeStruct((t, LANES), F32)],
        scratch_shapes=[pltpu.VMEM((tm, d), BF16)],
        compiler_params=_params(2),
        name="in_proj",
    )(x, g, w, b, wf, bf)


def _conv_kernel(a_ref, b_ref, hist_ref, w_ref, cb_ref, lg_ref, lb_ref, act_ref, buf_ref, ext_scr,
                 *, tt, n_tiles, width, sub, pad):
    i = pl.program_id(1)
    hist = width - 1
    off = pad - hist

    @pl.when(i == 0)
    def _():
        ext_scr[off:pad, :] = hist_ref[0]

    a = a_ref[...]
    ext_scr[pad:pad + tt, :] = a * _sigmoid(b_ref[...])
    for r0 in range(0, tt, sub):
        acc = ext_scr[r0 + off:r0 + off + sub, :] * w_ref[0:1, :]
        for j in range(1, width):
            acc = acc + ext_scr[r0 + off + j:r0 + off + j + sub, :] * w_ref[j:j + 1, :]
        y = acc + cb_ref[...]
        yc = y - jnp.mean(y, axis=-1, keepdims=True)
        yn = yc * lax.rsqrt(jnp.mean(yc * yc, axis=-1, keepdims=True) + EPS) * lg_ref[...] + lb_ref[...]
        act_ref[r0:r0 + sub, :] = (yn * _sigmoid(yn)).astype(BF16)

    @pl.when(i == n_tiles - 1)
    def _():
        buf_ref[0] = ext_scr[tt + off:tt + pad, :]

    if n_tiles > 1:
        ext_scr[off:pad, :] = ext_scr[tt + off:tt + pad, :]


def _conv_branch(z, hist, w, cb, lg, lb, bsz, seq):
    c = w.shape[1]
    width = w.shape[0]
    tt = _pick(seq, (256, 128, 64, 32, 16, 8))
    n_tiles = seq // tt
    sub = min(tt, 32)
    pad = 32
    assert width - 1 <= pad and (n_tiles == 1 or tt >= width - 1)
    kern = functools.partial(_conv_kernel, tt=tt, n_tiles=n_tiles, width=width, sub=sub, pad=pad)
    return pl.pallas_call(
        kern,
        grid=(bsz, n_tiles),
        in_specs=[
            pl.BlockSpec((tt, c), lambda b, i: (b * n_tiles + i, 0)),
            pl.BlockSpec((tt, c), lambda b, i: (b * n_tiles + i, 1)),
            pl.BlockSpec((1, width - 1, c), lambda b, i: (b, 0, 0)),
            pl.BlockSpec((width, c), lambda b, i: (0, 0)),
            pl.BlockSpec((1, c), lambda b, i: (0, 0)),
            pl.BlockSpec((1, c), lambda b, i: (0, 0)),
            pl.BlockSpec((1, c), lambda b, i: (0, 0)),
        ],
        out_specs=[
            pl.BlockSpec((tt, c), lambda b, i: (b * n_tiles + i, 0)),
            pl.BlockSpec((1, width - 1, c), lambda b, i: (b, 0, 0)),
        ],
        out_shape=[jax.ShapeDtypeStruct((bsz * seq, c), BF16),
                   jax.ShapeDtypeStruct((bsz, width - 1, c), F32)],
        scratch_shapes=[pltpu.VMEM((pad + tt, c), F32)],
        compiler_params=_params(2),
        name="conv_branch",
    )(z, z, hist, w, cb, lg, lb)


def _ret_kernel(q_ref, k_ref, v_ref, rg_ref, cos_ref, sin_ref, dm_ref, qd_ref, kd_ref, cd_ref, gn_ref,
                s0_ref, act_ref, s_ref, *, heads, dk, dv):
    @pl.when(pl.program_id(1) == 0)
    def _():
        s_ref[...] = s0_ref[...]

    cos = cos_ref[...]
    sin = sin_ref[...]
    half = dk // 2

    def rot(x):
        lane = lax.broadcasted_iota(jnp.int32, x.shape, 1)
        upper = (lane % dk) >= half
        swapped = jnp.where(upper, pltpu.roll(x, half, 1), pltpu.roll(x, x.shape[1] - half, 1))
        return x * cos + swapped * sin

    q = rot(q_ref[...]) * (dk ** -0.5)
    k = rot(k_ref[...])
    qb = q.astype(BF16)
    kb = k.astype(BF16)
    kdt = (k * kd_ref[...]).T.astype(BF16)
    vb = v_ref[...].astype(BF16)
    outs = []
    for h in range(heads):
        qh = qb[:, h * dk:(h + 1) * dk]
        kh = kb[:, h * dk:(h + 1) * dk]
        vh = vb[:, h * dv:(h + 1) * dv]
        inner = lax.dot_general(qh, kh, _NT, preferred_element_type=F32) * dm_ref[h]
        s_old = s_ref[0, h]
        o = _dot(inner.astype(BF16), vh) + _dot(qh, s_old.astype(BF16)) * qd_ref[:, h * dv:(h + 1) * dv]
        s_ref[0, h] = s_old * cd_ref[h] + _dot(kdt[h * dk:(h + 1) * dk, :], vh)
        outs.append(_rms(o, gn_ref[:, h * dv:(h + 1) * dv]))
    o_all = jnp.concatenate(outs, axis=1)
    rg = rg_ref[...]
    act_ref[...] = (rg * _sigmoid(rg) * o_all).astype(BF16)


def _ret_tables(seq, pos0, heads, dk, dv):
    c = math.gcd(seq, RET_CHUNK)
    half = dk // 2
    inv_freq = jnp.exp(-math.log(ROPE_BASE) * jnp.arange(half, dtype=F32) / half)
    pos = (pos0 + jnp.arange(seq)).astype(F32)
    ang = pos[:, None] * inv_freq[None, :]
    cos = jnp.tile(jnp.concatenate([jnp.cos(ang), jnp.cos(ang)], axis=-1), (1, heads))
    sin = jnp.tile(jnp.concatenate([-jnp.sin(ang), jnp.sin(ang)], axis=-1), (1, heads))
    log_g = jnp.log1p(-jnp.exp2(-5.0 - jnp.arange(heads, dtype=F32)))
    idx = jnp.arange(c, dtype=F32)
    diff = idx[:, None] - idx[None, :]
    dmask = jnp.where(diff >= 0, jnp.exp(log_g[:, None, None] * jnp.maximum(diff, 0.0)), 0.0)
    q_dec = jnp.exp(log_g[:, None] * (idx[None, :] + 1.0))
    k_dec = jnp.exp(log_g[:, None] * (c - 1.0 - idx[None, :]))
    chunk_dec = jnp.exp(log_g * c)
    qd = jnp.repeat(q_dec.T, dv, axis=1)
    kd = jnp.repeat(k_dec.T, dk, axis=1)
    cd = jnp.broadcast_to(chunk_dec[:, None, None], (heads, 1, dv))
    return c, cos, sin, dmask, qd, kd, cd


def _ret_branch(z, tables, gn, state0, bsz, seq, col0):
    heads, dk, dv = state0.shape[1], state0.shape[2], state0.shape[3]
    c, cos, sin, dmask, qd, kd, cd = tables
    n = seq // c
    hk, hv = heads * dk, heads * dv
    qcol, kcol, vcol, gcol = col0 // hk, col0 // hk + 1, (col0 + 2 * hk) // hv, (col0 + 2 * hk) // hv + 1
    kern = functools.partial(_ret_kernel, heads=heads, dk=dk, dv=dv)
    row = lambda b, i: b * n + i
    return pl.pallas_call(
        kern,
        grid=(bsz, n),
        in_specs=[
            pl.BlockSpec((c, hk), lambda b, i: (row(b, i), qcol)),
            pl.BlockSpec((c, hk), lambda b, i: (row(b, i), kcol)),
            pl.BlockSpec((c, hv), lambda b, i: (row(b, i), vcol)),
            pl.BlockSpec((c, hv), lambda b, i: (row(b, i), gcol)),
            pl.BlockSpec((c, hk), lambda b, i: (i, 0)),
            pl.BlockSpec((c, hk), lambda b, i: (i, 0)),
            pl.BlockSpec((heads, c, c), lambda b, i: (0, 0, 0)),
            pl.BlockSpec((c, hv), lambda b, i: (0, 0)),
            pl.BlockSpec((c, hk), lambda b, i: (0, 0)),
            pl.BlockSpec((heads, 1, dv), lambda b, i: (0, 0, 0)),
            pl.BlockSpec((1, hv), lambda b, i: (0, 0)),
            pl.BlockSpec((1, heads, dk, dv), lambda b, i: (b, 0, 0, 0)),
        ],
        out_specs=[
            pl.BlockSpec((c, hv), lambda b, i: (row(b, i), 0)),
            pl.BlockSpec((1, heads, dk, dv), lambda b, i: (b, 0, 0, 0)),
        ],
        out_shape=[jax.ShapeDtypeStruct((bsz * seq, hv), BF16),
                   jax.ShapeDtypeStruct(state0.shape, F32)],
        compiler_params=_params(2),
        name="retention",
    )(z, z, z, z, cos, sin, dmask, qd, kd, cd, gn, state0)


def _fox_prep_kernel(fq_ref, fk_ref, fv_ref, lf_ref, gq_ref, gk_ref, gm_ref, tri_ref, *refs, hd, with_ct):
    if with_ct:
        qn_ref, kf_ref, kb_ref, vf_ref, vb_ref, c_ref, ct_ref, carry = refs
    else:
        qn_ref, kf_ref, kb_ref, vf_ref, vb_ref, c_ref, carry = refs

    @pl.when(pl.program_id(1) == 0)
    def _():
        carry[...] = jnp.zeros_like(carry)

    gm = gm_ref[...]

    def head_norm(x, g):
        hi, lo = _split2(x * x)
        ms = (_dot(hi, gm) + _dot(lo, gm)) * (1.0 / hd)
        return x * lax.rsqrt(ms + EPS) * g

    qn_ref[...] = (head_norm(fq_ref[...], gq_ref[...]) * (hd ** -0.5)).astype(BF16)
    kn = head_norm(fk_ref[...], gk_ref[...])
    kf_ref[...] = kn
    kb_ref[...] = kn.astype(BF16)
    v = fv_ref[...]
    vf_ref[...] = v
    vb_ref[...] = v.astype(BF16)
    tri = tri_ref[...]
    h1, h2, h3 = _split3(lf_ref[...])
    cs = _dot(tri, h1) + _dot(tri, h2) + _dot(tri, h3) + carry[0:1, :]
    c_ref[...] = cs
    tm = cs.shape[0]
    carry[...] = jnp.broadcast_to(cs[tm - 1:tm, :], carry.shape)
    if with_ct:
        ct_ref[0] = cs.T[0:SUBLANES, :]


def _fox_prep(z, lf, gq, gk, bsz, seq, col0, heads, hd, with_ct):
    t = bsz * seq
    w = heads * hd
    tm = _pick(seq, (512, 256, 128, 64, 32, 16, 8))
    n = seq // tm
    qcol = col0 // w
    gm = jnp.kron(jnp.eye(heads, dtype=F32), jnp.ones((hd, hd), F32)).astype(BF16)
    tri = jnp.tril(jnp.ones((tm, tm), F32)).astype(BF16)
    row = lambda b, i: (b * n + i, 0)
    out_specs = [pl.BlockSpec((tm, w), row)] * 5 + [pl.BlockSpec((tm, LANES), row)]
    out_shape = [jax.ShapeDtypeStruct((t, w), BF16), jax.ShapeDtypeStruct((t, w), F32),
                 jax.ShapeDtypeStruct((t, w), BF16), jax.ShapeDtypeStruct((t, w), F32),
                 jax.ShapeDtypeStruct((t, w), BF16), jax.ShapeDtypeStruct((t, LANES), F32)]
    if with_ct:
        out_specs.append(pl.BlockSpec((1, SUBLANES, tm), lambda b, i: (b, 0, i)))
        out_shape.append(jax.ShapeDtypeStruct((bsz, SUBLANES, seq), F32))
    kern = functools.partial(_fox_prep_kernel, hd=hd, with_ct=with_ct)
    return pl.pallas_call(
        kern,
        grid=(bsz, n),
        in_specs=[
            pl.BlockSpec((tm, w), lambda b, i: (b * n + i, qcol)),
            pl.BlockSpec((tm, w), lambda b, i: (b * n + i, qcol + 1)),
            pl.BlockSpec((tm, w), lambda b, i: (b * n + i, qcol + 2)),
            pl.BlockSpec((tm, LANES), row),
            pl.BlockSpec((1, w), lambda b, i: (0, 0)),
            pl.BlockSpec((1, w), lambda b, i: (0, 0)),
            pl.BlockSpec((w, w), lambda b, i: (0, 0)),
            pl.BlockSpec((tm, tm), lambda b, i: (0, 0)),
        ],
        out_specs=out_specs,
        out_shape=out_shape,
        scratch_shapes=[pltpu.VMEM((SUBLANES, LANES), F32)],
        compiler_params=_params(2),
        name="fox_prep",
    )(z, z, z, lf, gq, gk, gm, tri)


def _fox_flash_kernel(q_ref, k_ref, v_ref, c_ref, ct_ref, o_ref, *, tq, hd):
    i = pl.program_id(2)
    hp = pl.program_id(1)
    lane = lax.broadcasted_iota(jnp.int32, (tq, LANES), 1)
    c_tile = c_ref[...]
    first = lane < hd
    q = q_ref[...].astype(F32)
    qs = [jnp.where(first, q, 0.0).astype(BF16), jnp.where(first, 0.0, q).astype(BF16)]
    cq = [jnp.sum(jnp.where(lane == 2 * hp + e, c_tile, 0.0), axis=-1, keepdims=True) for e in range(2)]

    def block(j, carry, masked):
        m, l, acc = carry
        start = pl.multiple_of(j * tq, tq)
        kj = k_ref[pl.ds(start, tq), :]
        vj = v_ref[pl.ds(start, tq), :]
        new_m, new_l, upd = [], [], []
        for e in range(2):
            s = lax.dot_general(qs[e], kj, _NT, preferred_element_type=F32)
            s = s - ct_ref[0, 0, e:e + 1, pl.ds(start, tq)]
            if masked:
                r = lax.broadcasted_iota(jnp.int32, (tq, tq), 0)
                col = lax.broadcasted_iota(jnp.int32, (tq, tq), 1)
                s = jnp.where(col <= r, s, NEG_INF)
            mb = jnp.max(s, axis=-1, keepdims=True) + cq[e]
            mn = jnp.maximum(m[e], mb)
            p = jnp.exp(s + (cq[e] - mn))
            a = jnp.exp(m[e] - mn)
            new_m.append(mn)
            new_l.append(a * l[e] + jnp.sum(p, axis=-1, keepdims=True))
            upd.append(acc * a + _dot(p.astype(BF16), vj))
        return tuple(new_m), tuple(new_l), jnp.where(first, upd[0], upd[1])

    init = ((jnp.full((tq, 1), NEG_INF, F32),) * 2, (jnp.zeros((tq, 1), F32),) * 2,
            jnp.zeros((tq, LANES), F32))
    carry = lax.fori_loop(0, i, functools.partial(block, masked=False), init)
    m, l, acc = block(i, carry, True)
    o_ref[...] = (acc / jnp.where(first, l[0], l[1])).astype(BF16)


def _fox_flash(qn, kb, vb, c, ct, bsz, seq, heads, hd):
    t = bsz * seq
    tq = _pick(seq, (512, 256, 128))
    nq = seq // tq
    pairs = heads // 2
    ct4 = ct.reshape(bsz, SUBLANES // 2, 2, seq)
    kern = functools.partial(_fox_flash_kernel, tq=tq, hd=hd)
    return pl.pallas_call(
        kern,
        grid=(bsz, pairs, nq),
        in_specs=[
            pl.BlockSpec((tq, LANES), lambda b, h, i: (b * nq + i, h)),
            pl.BlockSpec((seq, LANES), lambda b, h, i: (b, h)),
            pl.BlockSpec((seq, LANES), lambda b, h, i: (b, h)),
            pl.BlockSpec((tq, LANES), lambda b, h, i: (b * nq + i, 0)),
            pl.BlockSpec((1, 1, 2, seq), lambda b, h, i: (b, h, 0, 0)),
        ],
        out_specs=pl.BlockSpec((tq, LANES), lambda b, h, i: (b * nq + i, h)),
        out_shape=jax.ShapeDtypeStruct((t, heads * hd), BF16),
        compiler_params=_params(3),
        name="fox_flash",
    )(qn, kb, vb, c, ct4)


def _pool_scan_kernel(lf_ref, mx_ref, mt_ref, within_ref, tot_ref):
    hi, lo = _split2(lf_ref[...])
    within_ref[...] = _dot(hi, mx_ref[...]) + _dot(lo, mx_ref[...])
    tot_ref[...] = _dot(hi, mt_ref[...]) + _dot(lo, mt_ref[...])


def _pool_scan(lf_pool, heads):
    n_pool, w = lf_pool.shape
    pos = jnp.arange(w) // heads
    head = jnp.arange(w) % heads
    same = head[:, None] == head[None, :]
    mx = (same & (pos[:, None] > pos[None, :])).astype(BF16)
    mt = same.astype(BF16)
    tm = _pick(n_pool, (512, 256, 128, 64, 32, 16, 8))
    return pl.pallas_call(
        _pool_scan_kernel,
        grid=(n_pool // tm,),
        in_specs=[pl.BlockSpec((tm, w), lambda i: (i, 0)),
                  pl.BlockSpec((w, w), lambda i: (0, 0)),
                  pl.BlockSpec((w, w), lambda i: (0, 0))],
        out_specs=[pl.BlockSpec((tm, w), lambda i: (i, 0))] * 2,
        out_shape=[jax.ShapeDtypeStruct((n_pool, w), F32)] * 2,
        compiler_params=_params(1),
        name="pool_scan",
    )(lf_pool, mx, mt)


def _fox_paged_kernel(pt_ref, q_ref, kn_ref, vn_ref, cn_ref, *refs, g, heads, hd, n_new, n_chunks):
    k_refs = refs[0:g]
    v_refs = refs[g:2 * g]
    w_refs = refs[2 * g:3 * g]
    t_refs = refs[3 * g:4 * g]
    o_ref, m_scr, l_scr, acc_scr, carry_scr, q_scr, rowc_scr = refs[4 * g:]
    c = pl.program_id(1)
    rows = heads * n_new
    page = k_refs[0].shape[0]
    ncol = page * heads

    def by_head(x):
        return jnp.concatenate([x[:, h * hd:(h + 1) * hd] for h in range(heads)], axis=0)

    @pl.when(c == 0)
    def _():
        qh = by_head(q_ref[...].astype(F32)).astype(BF16)
        q_scr[...] = qh
        kh = by_head(kn_ref[...].astype(F32)).astype(BF16)
        vh = by_head(vn_ref[...].astype(F32)).astype(BF16)
        cn = cn_ref[...]
        lane = lax.broadcasted_iota(jnp.int32, (rows, LANES), 1)
        rh = lax.broadcasted_iota(jnp.int32, (rows, LANES), 0) // n_new
        rowc = jnp.sum(jnp.where(lane == rh, jnp.concatenate([cn] * heads, axis=0), 0.0),
                       axis=-1, keepdims=True)
        rowc_scr[...] = jnp.broadcast_to(rowc, rowc_scr.shape)
        ri = lax.broadcasted_iota(jnp.int32, (rows, rows), 0)
        ci = lax.broadcasted_iota(jnp.int32, (rows, rows), 1)
        d1, d2, d3 = _split3(jnp.where(ri == ci, rowc, 0.0))
        ones = jnp.ones((rows, rows), BF16)
        colc = _dot(ones, d1) + _dot(ones, d2) + _dot(ones, d3)
        s = lax.dot_general(qh, kh, _NT, preferred_element_type=F32) + rowc - colc
        same_head = (ri // n_new) == (ci // n_new)
        s = jnp.where(same_head, jnp.where((ci % n_new) <= (ri % n_new), s, NEG_INF), NEG_INF)
        m = jnp.max(s, axis=-1, keepdims=True)
        p = jnp.exp(s - m)
        m_scr[...] = jnp.broadcast_to(m, m_scr.shape)
        l_scr[...] = jnp.broadcast_to(jnp.sum(p, axis=-1, keepdims=True), l_scr.shape)
        acc_scr[...] = _dot(p.astype(BF16), vh)
        carry_scr[...] = jnp.zeros_like(carry_scr)

    qh = q_scr[...]
    rowc = rowc_scr[:, 0:1]
    carry = carry_scr[...]
    own = (lax.broadcasted_iota(jnp.int32, (rows, ncol), 1) % heads
           == lax.broadcasted_iota(jnp.int32, (rows, ncol), 0) // n_new)
    scores = []
    for j in range(g):
        kj = k_refs[j][...].reshape(ncol, hd).astype(BF16)
        bias = w_refs[j][...] + carry
        carry = carry + t_refs[j][...]
        s = lax.dot_general(qh, kj, _NT, preferred_element_type=F32) + rowc + bias
        scores.append(jnp.where(own, s, NEG_INF))
    carry_scr[...] = carry
    s = jnp.concatenate(scores, axis=1)
    m_old = m_scr[:, 0:1]
    m_new = jnp.maximum(m_old, jnp.max(s, axis=-1, keepdims=True))
    p = jnp.exp(s - m_new)
    alpha = jnp.exp(m_old - m_new)
    l_scr[...] = jnp.broadcast_to(alpha * l_scr[:, 0:1] + jnp.sum(p, axis=-1, keepdims=True), l_scr.shape)
    m_scr[...] = jnp.broadcast_to(m_new, m_scr.shape)
    pb = p.astype(BF16)
    acc = acc_scr[...] * alpha
    for j in range(g):
        acc = acc + _dot(pb[:, j * ncol:(j + 1) * ncol], v_refs[j][...].reshape(ncol, hd).astype(BF16))
    acc_scr[...] = acc

    @pl.when(c == n_chunks - 1)
    def _():
        on = acc / l_scr[:, 0:1]
        o_ref[...] = jnp.concatenate([on[h * n_new:(h + 1) * n_new, :] for h in range(heads)],
                                     axis=1).astype(BF16)


def _fox_paged(qn, kb, vb, c_new, k_cache, v_cache, layer, within, tot, page_table, n_new):
    bsz, n_pages = page_table.shape
    _, n_pool, page, heads, hd = k_cache.shape
    w = heads * hd
    ncol = page * heads
    g = _pick(n_pages, (8, 4, 2, 1))
    n_chunks = n_pages // g
    rows = heads * n_new
    pt = page_table.reshape(-1)

    def page_of(b, c, pt_ref, j):
        return pt_ref[b * n_pages + n_pages - 1 - (c * g + j)]

    def cache_map(j):
        return lambda b, c, pt_ref: (layer, page_of(b, c, pt_ref, j), 0, 0, 0)

    def row_map(j):
        return lambda b, c, pt_ref: (page_of(b, c, pt_ref, j), 0, 0)

    tok = lambda b, c, pt_ref: (b, 0)
    in_specs = [pl.BlockSpec((n_new, w), tok), pl.BlockSpec((n_new, w), tok), pl.BlockSpec((n_new, w), tok),
                pl.BlockSpec((n_new, LANES), tok)]
    in_specs += [pl.BlockSpec((None, None, page, heads, hd), cache_map(j)) for j in range(g)]
    in_specs += [pl.BlockSpec((None, None, page, heads, hd), cache_map(j)) for j in range(g)]
    in_specs += [pl.BlockSpec((None, 1, ncol), row_map(j)) for j in range(g)]
    in_specs += [pl.BlockSpec((None, 1, ncol), row_map(j)) for j in range(g)]
    kern = functools.partial(_fox_paged_kernel, g=g, heads=heads, hd=hd, n_new=n_new, n_chunks=n_chunks)
    grid_spec = pltpu.PrefetchScalarGridSpec(
        num_scalar_prefetch=1,
        grid=(bsz, n_chunks),
        in_specs=in_specs,
        out_specs=pl.BlockSpec((n_new, w), tok),
        scratch_shapes=[pltpu.VMEM((rows, LANES), F32), pltpu.VMEM((rows, LANES), F32),
                        pltpu.VMEM((rows, hd), F32), pltpu.VMEM((1, ncol), F32),
                        pltpu.VMEM((rows, hd), BF16), pltpu.VMEM((rows, LANES), F32)],
    )
    return pl.pallas_call(
        kern,
        grid_spec=grid_spec,
        out_shape=jax.ShapeDtypeStruct((bsz * n_new, w), BF16),
        compiler_params=_params(2),
        name="fox_paged",
    )(pt, qn, kb, vb, c_new, *([k_cache] * g), *([v_cache] * g), *([within] * g), *([tot] * g))


def _merge_kernel(x_ref, ca_ref, ra_ref, fo_ref, g0_ref, g1_ref, g2_ref, wpw_ref, bpw_ref, wro_ref, wfo_ref,
                  wout_ref, o_ref):
    y_conv = _dot(ca_ref[...], wpw_ref[...]) + bpw_ref[...]
    y_ret = _dot(ra_ref[...], wro_ref[...])
    y_fox = _dot(fo_ref[...], wfo_ref[...])
    merged = _sigmoid(g0_ref[...]) * y_conv + _sigmoid(g1_ref[...]) * y_ret + _sigmoid(g2_ref[...]) * y_fox
    o_ref[...] = x_ref[...] + _dot(merged.astype(BF16), wout_ref[...])


def _merge(x, z, ca, ra, fo, wpw, bpw, wro, wfo, wout, gate_col0):
    t, d = x.shape
    cw = ca.shape[1]
    tm = _pick(t, (512, 256))
    gc = gate_col0 // d
    row = lambda i: (i, 0)
    full = lambda i: (0, 0)
    return pl.pallas_call(
        _merge_kernel,
        grid=(t // tm,),
        in_specs=[
            pl.BlockSpec((tm, d), row),
            pl.BlockSpec((tm, cw), row), pl.BlockSpec((tm, cw), row), pl.BlockSpec((tm, cw), row),
            pl.BlockSpec((tm, d), lambda i: (i, gc)),
            pl.BlockSpec((tm, d), lambda i: (i, gc + 1)),
            pl.BlockSpec((tm, d), lambda i: (i, gc + 2)),
            pl.BlockSpec((cw, d), full), pl.BlockSpec((1, d), full),
            pl.BlockSpec((cw, d), full), pl.BlockSpec((cw, d), full), pl.BlockSpec((d, d), full),
        ],
        out_specs=pl.BlockSpec((tm, d), row),
        out_shape=jax.ShapeDtypeStruct((t, d), F32),
        compiler_params=_params(1),
        name="merge",
    )(x, ca, ra, fo, z, z, z, wpw, bpw, wro, wfo, wout)


def _mem_kv_kernel(m_ref, g_ref, wk_ref, wv_ref, gk_ref, k_ref, v_ref, *, heads, hd):
    mb = _rms(m_ref[...], g_ref[...]).astype(BF16)
    k = _dot(mb, wk_ref[...])
    k_ref[...] = jnp.concatenate(
        [_rms(k[:, h * hd:(h + 1) * hd], gk_ref[...]) for h in range(heads)], axis=1)
    v_ref[...] = _dot(mb, wv_ref[...])


def _mem_kv(mem, g, wk, wv, gk, heads, hd):
    t, d = mem.shape
    w = heads * hd
    tm = _pick(t, (512, 256, 128))
    full = lambda i: (0, 0)
    return pl.pallas_call(
        functools.partial(_mem_kv_kernel, heads=heads, hd=hd),
        grid=(t // tm,),
        in_specs=[pl.BlockSpec((tm, d), lambda i: (i, 0)), pl.BlockSpec((1, d), full),
                  pl.BlockSpec((d, w), full), pl.BlockSpec((d, w), full), pl.BlockSpec((1, hd), full)],
        out_specs=[pl.BlockSpec((tm, w), lambda i: (i, 0))] * 2,
        out_shape=[jax.ShapeDtypeStruct((t, w), F32)] * 2,
        compiler_params=_params(1),
        name="mem_kv",
    )(mem, g, wk, wv, gk)


def _cross_kernel(x_ref, gx_ref, wq_ref, gq_ref, mk_ref, mv_ref, wo_ref, o_ref, *, heads, hd, by_head):
    x = x_ref[...]
    q = _dot(_rms(x, gx_ref[...]).astype(BF16), wq_ref[...])
    outs = []
    for h in range(heads):
        if by_head:
            mk = mk_ref[:, h, :].astype(BF16)
            mv = mv_ref[:, h, :].astype(BF16)
        else:
            mk = mk_ref[:, h * hd:(h + 1) * hd].astype(BF16)
            mv = mv_ref[:, h * hd:(h + 1) * hd].astype(BF16)
        qh = _rms(q[:, h * hd:(h + 1) * hd], gq_ref[...]).astype(BF16)
        s = lax.dot_general(qh, mk, _NT, preferred_element_type=F32) * (hd ** -0.5)
        p = jnp.exp(s - jnp.max(s, axis=-1, keepdims=True))
        p = p / jnp.sum(p, axis=-1, keepdims=True)
        outs.append(_dot(p.astype(BF16), mv))
    o = jnp.concatenate(outs, axis=1).astype(BF16)
    o_ref[...] = x + _dot(o, wo_ref[...])


def _cross(x, gx, wq, gq, mem_k, mem_v, layer, wo, bsz, seq, heads, hd):
    t, d = x.shape
    w = heads * hd
    tm = _pick(seq, (512, 256, 128, 64, 32, 16, 8))
    n = seq // tm
    full = lambda b, i: (0, 0)
    by_head = mem_k.ndim == 5
    if by_head:
        n_mem = mem_k.shape[2]
        mem_spec = pl.BlockSpec((None, None, n_mem, heads, hd), lambda b, i: (layer, b, 0, 0, 0))
    else:
        n_mem = mem_k.shape[1]
        mem_spec = pl.BlockSpec((None, n_mem, w), lambda b, i: (b, 0, 0))
    return pl.pallas_call(
        functools.partial(_cross_kernel, heads=heads, hd=hd, by_head=by_head),
        grid=(bsz, n),
        in_specs=[pl.BlockSpec((tm, d), lambda b, i: (b * n + i, 0)), pl.BlockSpec((1, d), full),
                  pl.BlockSpec((d, w), full), pl.BlockSpec((1, hd), full),
                  mem_spec, mem_spec,
                  pl.BlockSpec((w, d), full)],
        out_specs=pl.BlockSpec((tm, d), lambda b, i: (b * n + i, 0)),
        out_shape=jax.ShapeDtypeStruct((t, d), F32),
        compiler_params=_params(2),
        name="cross_attn",
    )(x, gx, wq, gq, mem_k, mem_v, wo)


def _ffn_kernel(x_ref, g_ref, w1_ref, w2_ref, o_ref, h_scr, acc_scr):
    k = pl.program_id(1)

    @pl.when(k == 0)
    def _():
        h_scr[...] = _rms(x_ref[...], g_ref[...]).astype(BF16)
        acc_scr[...] = jnp.zeros_like(acc_scr)

    a = jnp.maximum(_dot(h_scr[...], w1_ref[...]), 0.0)
    acc_scr[...] += _dot((a * a).astype(BF16), w2_ref[...])

    @pl.when(k == pl.num_programs(1) - 1)
    def _():
        o_ref[...] = x_ref[...] + acc_scr[...]


def _ffn(x, g, w1, w2):
    t, d = x.shape
    dff = w1.shape[1]
    tm = _pick(t, (512, 256))
    tf = 1024
    return pl.pallas_call(
        _ffn_kernel,
        grid=(t // tm, dff // tf),
        in_specs=[pl.BlockSpec((tm, d), lambda i, k: (i, 0)), pl.BlockSpec((1, d), lambda i, k: (0, 0)),
                  pl.BlockSpec((d, tf), lambda i, k: (0, k)), pl.BlockSpec((tf, d), lambda i, k: (k, 0))],
        out_specs=pl.BlockSpec((tm, d), lambda i, k: (i, 0)),
        out_shape=jax.ShapeDtypeStruct((t, d), F32),
        scratch_shapes=[pltpu.VMEM((tm, d), BF16), pltpu.VMEM((tm, d), F32)],
        compiler_params=_params(2),
        name="ffn",
    )(x, g, w1, w2)


def _layer(x, bsz, seq, conv_hist, ret_state0, ret_tables, mem_k, mem_v, layer, lw, dims, paged):
    d_conv, ret_heads, ret_dk, ret_dv, fox_heads, fox_hd, x_heads, x_hd = dims
    z, lf = _in_proj(x, lw["g_mix"], lw["w_main"], lw["b_main"], lw["w_f"], lw["b_f"])
    c_act, new_buf = _conv_branch(z, conv_hist, lw["conv_w"], lw["conv_b"], lw["conv_ln_g"], lw["conv_ln_b"],
                                  bsz, seq)
    ret_col0 = 2 * d_conv
    r_act, new_state = _ret_branch(z, ret_tables, lw["ret_gn"], ret_state0, bsz, seq, ret_col0)
    fox_col0 = ret_col0 + 2 * ret_heads * ret_dk + 2 * ret_heads * ret_dv
    prep = _fox_prep(z, lf, lw["fox_g_q"], lw["fox_g_k"], bsz, seq, fox_col0, fox_heads, fox_hd,
                     with_ct=paged is None)
    if paged is None:
        qn, kf, kb, vf, vb, c, ct = prep
        fo = _fox_flash(qn, kb, vb, c, ct, bsz, seq, fox_heads, fox_hd)
    else:
        qn, kf, kb, vf, vb, c = prep
        k_cache, v_cache, within, tot, page_table = paged
        fo = _fox_paged(qn, kb, vb, c, k_cache, v_cache, layer, within, tot, page_table, seq)
    gate_col0 = fox_col0 + 3 * fox_heads * fox_hd
    x = _merge(x, z, c_act, r_act, fo, lw["conv_w_pw"], lw["conv_b_pw"], lw["ret_w_o"], lw["fox_w_o"],
               lw["w_out"], gate_col0)
    x = _cross(x, lw["g_x"], lw["x_w_q"], lw["x_g_q"], mem_k, mem_v, layer, lw["x_w_o"], bsz, seq, x_heads, x_hd)
    x = _ffn(x, lw["g_ff"], lw["w_ff1"], lw["w_ff2"])
    return x, new_buf, new_state, kf, vf, lf[:, :fox_heads]


def kernel(x_prompt, x_sample, cache_conv, state_ret, cache_fox_k, cache_fox_v, cache_fox_logf, cache_mem_k, cache_mem_v, page_table, mem_prompt, g_mix, w_in, b_in, conv_w, conv_b, conv_ln_g, conv_ln_b, conv_w_pw, conv_b_pw, ret_gn, ret_w_o, fox_g_q, fox_g_k, fox_w_o, w_out, g_x, g_mem, x_w_q, x_w_k, x_w_v, x_g_q, x_g_k, x_w_o, g_ff, w_ff1, w_ff2):
    bp, sp, d = x_prompt.shape
    bs, ss, _ = x_sample.shape
    depth = w_in.shape[0]
    width, d_conv = conv_w.shape[1], conv_w.shape[2]
    ret_heads, ret_dk, ret_dv = state_ret.shape[2], state_ret.shape[3], state_ret.shape[4]
    n_pool, page, fox_heads, fox_hd = cache_fox_k.shape[1:]
    n_mem, x_heads, x_hd = cache_mem_k.shape[2:]
    n_pages = page_table.shape[1]
    past = n_pages * page
    dims = (d_conv, ret_heads, ret_dk, ret_dv, fox_heads, fox_hd, x_heads, x_hd)
    fw = fox_heads * fox_hd
    forget_col = 2 * d_conv + 2 * ret_heads * ret_dk + 2 * ret_heads * ret_dv + 3 * fw

    tables_p = _ret_tables(sp, 0, ret_heads, ret_dk, ret_dv)
    tables_s = _ret_tables(ss, past, ret_heads, ret_dk, ret_dv)
    conv0 = jnp.zeros((bp, width - 1, d_conv), F32)
    ret0 = jnp.zeros((bp, ret_heads, ret_dk, ret_dv), F32)

    xp = x_prompt.reshape(bp * sp, d)
    xs = x_sample.reshape(bs * ss, d)
    mem2d = mem_prompt.reshape(bp * n_mem, d)
    outs_p = [[] for _ in range(7)]
    outs_s = [[] for _ in range(5)]
    row = lambda a: a.reshape(1, -1)
    for l in range(depth):
        w_f = jnp.pad(w_in[l][:, forget_col:forget_col + fox_heads], ((0, 0), (0, LANES - fox_heads)))
        b_f = jnp.pad(b_in[l][forget_col:forget_col + fox_heads], (0, LANES - fox_heads))
        lw = {
            "g_mix": row(g_mix[l]),
            "w_main": jnp.concatenate([w_in[l][:, :forget_col], w_in[l][:, forget_col + fox_heads:]],
                                      axis=1).astype(BF16),
            "b_main": row(jnp.concatenate([b_in[l][:forget_col], b_in[l][forget_col + fox_heads:]])),
            "w_f": w_f.astype(BF16), "b_f": row(b_f),
            "conv_w": conv_w[l], "conv_b": row(conv_b[l]),
            "conv_ln_g": row(conv_ln_g[l]), "conv_ln_b": row(conv_ln_b[l]),
            "conv_w_pw": conv_w_pw[l].astype(BF16), "conv_b_pw": row(conv_b_pw[l]),
            "ret_gn": row(ret_gn[l]), "ret_w_o": ret_w_o[l].astype(BF16),
            "fox_g_q": row(jnp.tile(fox_g_q[l], fox_heads)), "fox_g_k": row(jnp.tile(fox_g_k[l], fox_heads)),
            "fox_w_o": fox_w_o[l].astype(BF16), "w_out": w_out[l].astype(BF16),
            "g_x": row(g_x[l]), "x_w_q": x_w_q[l].astype(BF16), "x_g_q": row(x_g_q[l]),
            "x_w_o": x_w_o[l].astype(BF16), "g_ff": row(g_ff[l]),
            "w_ff1": w_ff1[l].astype(BF16), "w_ff2": w_ff2[l].astype(BF16),
        }
        mk, mv = _mem_kv(mem2d, row(g_mem[l]), x_w_k[l].astype(BF16), x_w_v[l].astype(BF16), row(x_g_k[l]),
                         x_heads, x_hd)
        mk3 = mk.reshape(bp, n_mem, x_heads * x_hd)
        mv3 = mv.reshape(bp, n_mem, x_heads * x_hd)
        xp, c_b, r_s, f_k, f_v, l_f = _layer(xp, bp, sp, conv0, ret0, tables_p, mk3, mv3, l, lw, dims, None)
        for lst, val in zip(outs_p, (c_b, r_s, f_k.reshape(bp, sp, fox_heads, fox_hd),
                                     f_v.reshape(bp, sp, fox_heads, fox_hd), l_f.reshape(bp, sp, fox_heads),
                                     mk.reshape(bp, n_mem, x_heads, x_hd), mv.reshape(bp, n_mem, x_heads, x_hd))):
            lst.append(val)
        within, tot = _pool_scan(cache_fox_logf[l].reshape(n_pool, page * fox_heads), fox_heads)
        paged = (cache_fox_k, cache_fox_v, within.reshape(n_pool, 1, page * fox_heads),
                 tot.reshape(n_pool, 1, page * fox_heads), page_table)
        xs, c_b, r_s, f_k, f_v, l_f = _layer(xs, bs, ss, cache_conv[l], state_ret[l], tables_s,
                                             cache_mem_k, cache_mem_v, l, lw, dims, paged)
        for lst, val in zip(outs_s, (c_b, r_s, f_k.reshape(bs, ss, fox_heads, fox_hd),
                                     f_v.reshape(bs, ss, fox_heads, fox_hd), l_f.reshape(bs, ss, fox_heads))):
            lst.append(val)
    return (xp.reshape(bp, sp, d), xs.reshape(bs, ss, d),
            *[jnp.stack(o) for o in outs_p], *[jnp.stack(o) for o in outs_s])
```

```python
import functools
import math

import jax
import jax.numpy as jnp
from jax import lax
from jax.experimental import pallas as pl
from jax.experimental.pallas import tpu as pltpu

F32 = jnp.float32
BF16 = jnp.bfloat16
EPS = 1e-6
ROPE_BASE = 10000.0
RET_CHUNK = 128
LOG2E = 1.4426950408889634
LANES = 128
SUBLANES = 8
BF16_ROWS = 16
VMEM_LIMIT = 56 * 1024 * 1024
NEG_INF = float("-inf")

_NT = (((1,), (1,)), ((), ()))


def _params(n_grid):
    return pltpu.CompilerParams(dimension_semantics=("arbitrary",) * n_grid,
                                vmem_limit_bytes=VMEM_LIMIT)


def _split2(x):
    hi = x.astype(BF16)
    lo = (x - hi.astype(F32)).astype(BF16)
    return hi, lo


def _split3(x):
    hi = x.astype(BF16)
    r = x - hi.astype(F32)
    mid = r.astype(BF16)
    lo = (r - mid.astype(F32)).astype(BF16)
    return hi, mid, lo


def _dot(a, b):
    return jnp.dot(a, b, preferred_element_type=F32)


def _rms(x, g):
    return x * lax.rsqrt(jnp.mean(x * x, axis=-1, keepdims=True) + EPS) * g


def _log_sigmoid(x):
    return jnp.minimum(x, 0.0) - jnp.log1p(jnp.exp(-jnp.abs(x)))


def _sigmoid(x):
    return 1.0 / (1.0 + jnp.exp(-x))


def _pick(n, cands):
    for c in cands:
        if n % c == 0:
            return c
    return n


def _head_norm(x, g, gm, hd):
    hi, lo = _split2(x * x)
    ms = (_dot(hi, gm) + _dot(lo, gm)) * (1.0 / hd)
    return x * lax.rsqrt(ms + EPS) * g


def _cumsum_rows(x, tri):
    h1, h2, h3 = _split3(x)
    return _dot(tri, h1) + _dot(tri, h2) + _dot(tri, h3)


def _in_proj_kernel(x_ref, g_ref, w_ref, b_ref, wf_ref, bf_ref, z_ref, lf_ref, h_scr):
    @pl.when(pl.program_id(1) == 0)
    def _():
        hb = _rms(x_ref[...], g_ref[...]).astype(BF16)
        h_scr[...] = hb
        lf_ref[...] = _log_sigmoid(_dot(hb, wf_ref[...]) + bf_ref[...])

    z_ref[...] = (_dot(h_scr[...], w_ref[...]) + b_ref[...]).astype(z_ref.dtype)


def _in_proj(x, g, w, b, wf, bf, z_dtype):
    t, d = x.shape
    n = w.shape[1]
    tm = _pick(t, (512, 256))
    tn = 1024
    return pl.pallas_call(
        _in_proj_kernel,
        grid=(t // tm, n // tn),
        in_specs=[
            pl.BlockSpec((tm, d), lambda i, j: (i, 0)),
            pl.BlockSpec((1, d), lambda i, j: (0, 0)),
            pl.BlockSpec((d, tn), lambda i, j: (0, j)),
            pl.BlockSpec((1, tn), lambda i, j: (0, j)),
            pl.BlockSpec((d, LANES), lambda i, j: (0, 0)),
            pl.BlockSpec((1, LANES), lambda i, j: (0, 0)),
        ],
        out_specs=[
            pl.BlockSpec((tm, tn), lambda i, j: (i, j)),
            pl.BlockSpec((tm, LANES), lambda i, j: (i, 0)),
        ],
        out_shape=[jax.ShapeDtypeStruct((t, n), z_dtype), jax.ShapeDtypeStruct((t, LANES), F32)],
        scratch_shapes=[pltpu.VMEM((tm, d), BF16)],
        compiler_params=_params(2),
        name="in_proj",
    )(x, g, w, b, wf, bf)


def _conv_kernel(a_ref, b_ref, hist_ref, w_ref, cb_ref, lg_ref, lb_ref, act_ref, buf_ref, ext_scr,
                 *, tt, n_tiles, width, sub, pad):
    i = pl.program_id(1)
    hist = width - 1
    off = pad - hist

    @pl.when(i == 0)
    def _():
        ext_scr[off:pad, :] = hist_ref[0]

    ext_scr[pad:pad + tt, :] = a_ref[...].astype(F32) * _sigmoid(b_ref[...].astype(F32))
    for r0 in range(0, tt, sub):
        acc = ext_scr[r0 + off:r0 + off + sub, :] * w_ref[0:1, :]
        for j in range(1, width):
            acc = acc + ext_scr[r0 + off + j:r0 + off + j + sub, :] * w_ref[j:j + 1, :]
        y = acc + cb_ref[...]
        yc = y - jnp.mean(y, axis=-1, keepdims=True)
        yn = yc * lax.rsqrt(jnp.mean(yc * yc, axis=-1, keepdims=True) + EPS) * lg_ref[...] + lb_ref[...]
        act_ref[r0:r0 + sub, :] = (yn * _sigmoid(yn)).astype(BF16)

    @pl.when(i == n_tiles - 1)
    def _():
        buf_ref[0] = ext_scr[tt + off:tt + pad, :]

    if n_tiles > 1:
        ext_scr[off:pad, :] = ext_scr[tt + off:tt + pad, :]


def _conv_branch(z, hist, w, cb, lg, lb, bsz, seq):
    c = w.shape[1]
    width = w.shape[0]
    tt = _pick(seq, (256, 128, 64, 32, 16, 8))
    n_tiles = seq // tt
    sub = min(tt, 32)
    pad = 32
    assert width - 1 <= pad and (n_tiles == 1 or tt >= width - 1)
    kern = functools.partial(_conv_kernel, tt=tt, n_tiles=n_tiles, width=width, sub=sub, pad=pad)
    return pl.pallas_call(
        kern,
        grid=(bsz, n_tiles),
        in_specs=[
            pl.BlockSpec((tt, c), lambda b, i: (b * n_tiles + i, 0)),
            pl.BlockSpec((tt, c), lambda b, i: (b * n_tiles + i, 1)),
            pl.BlockSpec((1, width - 1, c), lambda b, i: (b, 0, 0)),
            pl.BlockSpec((width, c), lambda b, i: (0, 0)),
            pl.BlockSpec((1, c), lambda b, i: (0, 0)),
            pl.BlockSpec((1, c), lambda b, i: (0, 0)),
            pl.BlockSpec((1, c), lambda b, i: (0, 0)),
        ],
        out_specs=[
            pl.BlockSpec((tt, c), lambda b, i: (b * n_tiles + i, 0)),
            pl.BlockSpec((1, width - 1, c), lambda b, i: (b, 0, 0)),
        ],
        out_shape=[jax.ShapeDtypeStruct((bsz * seq, c), BF16),
                   jax.ShapeDtypeStruct((bsz, width - 1, c), F32)],
        scratch_shapes=[pltpu.VMEM((pad + tt, c), F32)],
        compiler_params=_params(2),
        name="conv_branch",
    )(z, z, hist, w, cb, lg, lb)


def _ret_kernel(q_ref, k_ref, v_ref, rg_ref, cos_ref, sin_ref, dm_ref, qd_ref, kd_ref, cd_ref, gn_ref,
                s0_ref, act_ref, s_ref, *, heads, dk, dv):
    @pl.when(pl.program_id(1) == 0)
    def _():
        s_ref[...] = s0_ref[...]

    cos = cos_ref[...]
    sin = sin_ref[...]
    half = dk // 2

    def rot(x):
        lane = lax.broadcasted_iota(jnp.int32, x.shape, 1)
        upper = (lane % dk) >= half
        swapped = jnp.where(upper, pltpu.roll(x, half, 1), pltpu.roll(x, x.shape[1] - half, 1))
        return x * cos + swapped * sin

    q = rot(q_ref[...].astype(F32)) * (dk ** -0.5)
    k = rot(k_ref[...].astype(F32))
    qb = q.astype(BF16)
    kb = k.astype(BF16)
    kdt = (k * kd_ref[...]).T.astype(BF16)
    vb = v_ref[...].astype(BF16)
    outs = []
    for h in range(heads):
        qh = qb[:, h * dk:(h + 1) * dk]
        kh = kb[:, h * dk:(h + 1) * dk]
        vh = vb[:, h * dv:(h + 1) * dv]
        inner = lax.dot_general(qh, kh, _NT, preferred_element_type=F32) * dm_ref[h]
        s_old = s_ref[0, h]
        o = _dot(inner.astype(BF16), vh) + _dot(qh, s_old.astype(BF16)) * qd_ref[:, h * dv:(h + 1) * dv]
        s_ref[0, h] = s_old * cd_ref[h] + _dot(kdt[h * dk:(h + 1) * dk, :], vh)
        outs.append(_rms(o, gn_ref[:, h * dv:(h + 1) * dv]))
    o_all = jnp.concatenate(outs, axis=1)
    rg = rg_ref[...].astype(F32)
    act_ref[...] = (rg * _sigmoid(rg) * o_all).astype(BF16)


def _ret_tables(seq, pos0, heads, dk, dv):
    c = math.gcd(seq, RET_CHUNK)
    half = dk // 2
    inv_freq = jnp.exp(-math.log(ROPE_BASE) * jnp.arange(half, dtype=F32) / half)
    pos = (pos0 + jnp.arange(seq)).astype(F32)
    ang = pos[:, None] * inv_freq[None, :]
    cos = jnp.tile(jnp.concatenate([jnp.cos(ang), jnp.cos(ang)], axis=-1), (1, heads))
    sin = jnp.tile(jnp.concatenate([-jnp.sin(ang), jnp.sin(ang)], axis=-1), (1, heads))
    log_g = jnp.log1p(-jnp.exp2(-5.0 - jnp.arange(heads, dtype=F32)))
    idx = jnp.arange(c, dtype=F32)
    diff = idx[:, None] - idx[None, :]
    dmask = jnp.where(diff >= 0, jnp.exp(log_g[:, None, None] * jnp.maximum(diff, 0.0)), 0.0)
    q_dec = jnp.exp(log_g[:, None] * (idx[None, :] + 1.0))
    k_dec = jnp.exp(log_g[:, None] * (c - 1.0 - idx[None, :]))
    chunk_dec = jnp.exp(log_g * c)
    qd = jnp.repeat(q_dec.T, dv, axis=1)
    kd = jnp.repeat(k_dec.T, dk, axis=1)
    cd = jnp.broadcast_to(chunk_dec[:, None, None], (heads, 1, dv))
    return c, cos, sin, dmask, qd, kd, cd


def _ret_branch(z, tables, gn, state0, bsz, seq, col0):
    heads, dk, dv = state0.shape[1], state0.shape[2], state0.shape[3]
    c, cos, sin, dmask, qd, kd, cd = tables
    n = seq // c
    hk, hv = heads * dk, heads * dv
    qcol, kcol, vcol, gcol = col0 // hk, col0 // hk + 1, (col0 + 2 * hk) // hv, (col0 + 2 * hk) // hv + 1
    kern = functools.partial(_ret_kernel, heads=heads, dk=dk, dv=dv)
    row = lambda b, i: b * n + i
    return pl.pallas_call(
        kern,
        grid=(bsz, n),
        in_specs=[
            pl.BlockSpec((c, hk), lambda b, i: (row(b, i), qcol)),
            pl.BlockSpec((c, hk), lambda b, i: (row(b, i), kcol)),
            pl.BlockSpec((c, hv), lambda b, i: (row(b, i), vcol)),
            pl.BlockSpec((c, hv), lambda b, i: (row(b, i), gcol)),
            pl.BlockSpec((c, hk), lambda b, i: (i, 0)),
            pl.BlockSpec((c, hk), lambda b, i: (i, 0)),
            pl.BlockSpec((heads, c, c), lambda b, i: (0, 0, 0)),
            pl.BlockSpec((c, hv), lambda b, i: (0, 0)),
            pl.BlockSpec((c, hk), lambda b, i: (0, 0)),
            pl.BlockSpec((heads, 1, dv), lambda b, i: (0, 0, 0)),
            pl.BlockSpec((1, hv), lambda b, i: (0, 0)),
            pl.BlockSpec((1, heads, dk, dv), lambda b, i: (b, 0, 0, 0)),
        ],
        out_specs=[
            pl.BlockSpec((c, hv), lambda b, i: (row(b, i), 0)),
            pl.BlockSpec((1, heads, dk, dv), lambda b, i: (b, 0, 0, 0)),
        ],
        out_shape=[jax.ShapeDtypeStruct((bsz * seq, hv), BF16),
                   jax.ShapeDtypeStruct(state0.shape, F32)],
        compiler_params=_params(2),
        name="retention",
    )(z, z, z, z, cos, sin, dmask, qd, kd, cd, gn, state0)


def _fox_prep_kernel(fq_ref, fk_ref, fv_ref, lf_ref, gq_ref, gk_ref, gm_ref, tri_ref,
                     qn_ref, kf_ref, kb_ref, vf_ref, vb_ref, c_ref, carry, *, hd):
    @pl.when(pl.program_id(1) == 0)
    def _():
        carry[...] = jnp.zeros_like(carry)

    gm = gm_ref[...]
    qn_ref[...] = (_head_norm(fq_ref[...].astype(F32), gq_ref[...], gm, hd) * (hd ** -0.5)).astype(BF16)
    kn = _head_norm(fk_ref[...].astype(F32), gk_ref[...], gm, hd)
    kf_ref[...] = kn
    kb_ref[...] = kn.astype(BF16)
    v = fv_ref[...].astype(F32)
    vf_ref[...] = v
    vb_ref[...] = v.astype(BF16)
    cs = _cumsum_rows(lf_ref[...], tri_ref[...]) + carry[0:1, :]
    c_ref[...] = cs
    tm = cs.shape[0]
    carry[...] = jnp.broadcast_to(cs[tm - 1:tm, :], carry.shape)


def _fox_consts(heads, hd, tm):
    gm = jnp.kron(jnp.eye(heads, dtype=F32), jnp.ones((hd, hd), F32)).astype(BF16)
    tri = jnp.tril(jnp.ones((tm, tm), F32)).astype(BF16)
    return gm, tri


def _fox_prep(z, lf, gq, gk, bsz, seq, col0, heads, hd):
    t = bsz * seq
    w = heads * hd
    tm = _pick(seq, (512, 256, 128, 64, 32, 16, 8))
    n = seq // tm
    qcol = col0 // w
    gm, tri = _fox_consts(heads, hd, tm)
    row = lambda b, i: (b * n + i, 0)
    full = lambda b, i: (0, 0)
    return pl.pallas_call(
        functools.partial(_fox_prep_kernel, hd=hd),
        grid=(bsz, n),
        in_specs=[
            pl.BlockSpec((tm, w), lambda b, i: (b * n + i, qcol)),
            pl.BlockSpec((tm, w), lambda b, i: (b * n + i, qcol + 1)),
            pl.BlockSpec((tm, w), lambda b, i: (b * n + i, qcol + 2)),
            pl.BlockSpec((tm, LANES), row),
            pl.BlockSpec((1, w), full), pl.BlockSpec((1, w), full),
            pl.BlockSpec((w, w), full), pl.BlockSpec((tm, tm), full),
        ],
        out_specs=[pl.BlockSpec((tm, w), row)] * 5 + [pl.BlockSpec((tm, LANES), row)],
        out_shape=[jax.ShapeDtypeStruct((t, w), BF16), jax.ShapeDtypeStruct((t, w), F32),
                   jax.ShapeDtypeStruct((t, w), BF16), jax.ShapeDtypeStruct((t, w), F32),
                   jax.ShapeDtypeStruct((t, w), BF16), jax.ShapeDtypeStruct((t, LANES), F32)],
        scratch_shapes=[pltpu.VMEM((SUBLANES, LANES), F32)],
        compiler_params=_params(2),
        name="fox_prep",
    )(z, z, z, lf, gq, gk, gm, tri)


def _fox_prep_prompt_kernel(fq_ref, fk_ref, fv_ref, lf_ref, gq_ref, gk_ref, gm_ref, tri_ref, place_ref,
                            cq_ref, ck_ref, rows_ref, kbuf_ref, vbuf_ref,
                            q_ref, k_ref, v_ref, kt_ref, vt_ref, carry, *, hd):
    del kbuf_ref, vbuf_ref

    @pl.when(pl.program_id(1) == 0)
    def _():
        carry[...] = jnp.zeros_like(carry)

    gm = gm_ref[...]
    place = place_ref[...]
    qn = _head_norm(fq_ref[...].astype(F32), gq_ref[...], gm, hd) * (LOG2E * hd ** -0.5)
    kn = _head_norm(fk_ref[...].astype(F32), gk_ref[...], gm, hd)
    v = fv_ref[...].astype(F32)
    cs = _cumsum_rows(lf_ref[...], tri_ref[...]) + carry[0:1, :]
    tm = cs.shape[0]
    carry[...] = jnp.broadcast_to(cs[tm - 1:tm, :], carry.shape)
    c1, c2, c3 = _split3(cs * LOG2E)
    q_ref[...] = (_dot(qn.astype(BF16), place) + _dot(c1, cq_ref[0]) + _dot(c2, cq_ref[1])
                  + _dot(c3, cq_ref[2]) + rows_ref[0:1, :]).astype(BF16)
    k_ref[...] = (_dot(kn.astype(BF16), place) + _dot(c1, ck_ref[0]) + _dot(c2, ck_ref[1])
                  + _dot(c3, ck_ref[2]) + rows_ref[1:2, :]).astype(BF16)
    v_ref[...] = (_dot(v.astype(BF16), place) + rows_ref[2:3, :]).astype(BF16)
    kt_ref[...] = kn.T
    vt_ref[...] = v.T


def _fox_prep_prompt(z, lf, gq, gk, kbuf, vbuf, layer, bsz, seq, col0, heads, hd):
    t = bsz * seq
    w = heads * hd
    wide = heads * LANES
    tm = _pick(seq, (512, 256, 128))
    n = seq // tm
    qcol = col0 // w
    gm, tri = _fox_consts(heads, hd, tm)
    src = jnp.arange(w)
    place = (((src // hd) * LANES + src % hd)[:, None] == jnp.arange(wide)[None, :]).astype(BF16)
    head = jnp.arange(LANES)[:, None]
    dst = jnp.arange(wide)[None, :]

    def lane_sel(offset):
        return (dst == head * LANES + hd + offset) & (head < heads)

    cq = jnp.stack([lane_sel(i) for i in range(3)]).astype(BF16)
    ck = -jnp.stack([lane_sel(3 + i) for i in range(3)]).astype(BF16)
    lane_in_head = jnp.arange(wide) % LANES
    rows = jnp.stack([((lane_in_head >= hd + 3) & (lane_in_head < hd + 6)).astype(F32),
                      ((lane_in_head >= hd) & (lane_in_head < hd + 3)).astype(F32),
                      (lane_in_head == hd).astype(F32)] + [jnp.zeros((wide,), F32)] * (SUBLANES - 3))
    row = lambda b, i: (b * n + i, 0)
    full = lambda b, i: (0, 0)
    full3 = lambda b, i: (0, 0, 0)
    tspec = pl.BlockSpec((None, None, w, tm), lambda b, i: (layer, b, 0, i))
    return pl.pallas_call(
        functools.partial(_fox_prep_prompt_kernel, hd=hd),
        grid=(bsz, n),
        in_specs=[
            pl.BlockSpec((tm, w), lambda b, i: (b * n + i, qcol)),
            pl.BlockSpec((tm, w), lambda b, i: (b * n + i, qcol + 1)),
            pl.BlockSpec((tm, w), lambda b, i: (b * n + i, qcol + 2)),
            pl.BlockSpec((tm, LANES), row),
            pl.BlockSpec((1, w), full), pl.BlockSpec((1, w), full),
            pl.BlockSpec((w, w), full), pl.BlockSpec((tm, tm), full),
            pl.BlockSpec((w, wide), full),
            pl.BlockSpec((3, LANES, wide), full3), pl.BlockSpec((3, LANES, wide), full3),
            pl.BlockSpec((SUBLANES, wide), full),
            pl.BlockSpec(memory_space=pl.ANY), pl.BlockSpec(memory_space=pl.ANY),
        ],
        out_specs=[pl.BlockSpec((tm, wide), row)] * 3 + [tspec, tspec],
        out_shape=[jax.ShapeDtypeStruct((t, wide), BF16)] * 3
        + [jax.ShapeDtypeStruct(kbuf.shape, F32), jax.ShapeDtypeStruct(vbuf.shape, F32)],
        input_output_aliases={12: 3, 13: 4},
        scratch_shapes=[pltpu.VMEM((SUBLANES, LANES), F32)],
        compiler_params=_params(2),
        name="fox_prep_prompt",
    )(z, z, z, lf, gq, gk, gm, tri, place, cq, ck, rows, kbuf, vbuf)


def _fox_flash_kernel(q_ref, k_ref, v_ref, o_ref, *, tq, hd):
    i = pl.program_id(2)
    lane = lax.broadcasted_iota(jnp.int32, (tq, LANES), 1)

    def block(j, carry, masked):
        start = pl.multiple_of(j * tq, tq)
        out = []
        for e in range(2):
            m_old, acc = carry[e]
            cols = slice(e * LANES, (e + 1) * LANES)
            s = lax.dot_general(q_ref[:, cols], k_ref[pl.ds(start, tq), cols], _NT, preferred_element_type=F32)
            if masked:
                r = lax.broadcasted_iota(jnp.int32, (tq, tq), 0)
                c = lax.broadcasted_iota(jnp.int32, (tq, tq), 1)
                s = jnp.where(c <= r, s, NEG_INF)
            m_new = jnp.maximum(m_old, jnp.max(s, axis=-1, keepdims=True))
            p = jnp.exp2(s - m_new)
            acc = acc * jnp.exp2(m_old - m_new) + _dot(p.astype(BF16), v_ref[pl.ds(start, tq), cols])
            out.append((m_new, acc))
        return tuple(out)

    init = ((jnp.full((tq, 1), NEG_INF, F32), jnp.zeros((tq, LANES), F32)),) * 2
    carry = lax.fori_loop(0, i, functools.partial(block, masked=False), init)
    carry = block(i, carry, True)
    outs = []
    for e in range(2):
        acc = carry[e][1]
        denom = jnp.sum(jnp.where(lane == hd, acc, 0.0), axis=-1, keepdims=True)
        outs.append(acc / denom)
    o_ref[...] = jnp.where(lane < hd, outs[0], pltpu.roll(outs[1], hd, 1)).astype(BF16)


def _fox_flash(qw, kw, vw, bsz, seq, heads, hd):
    t = bsz * seq
    tq = _pick(seq, (512, 256, 128))
    nq = seq // tq
    kern = functools.partial(_fox_flash_kernel, tq=tq, hd=hd)
    return pl.pallas_call(
        kern,
        grid=(bsz, heads // 2, nq),
        in_specs=[
            pl.BlockSpec((tq, 2 * LANES), lambda b, h, i: (b * nq + i, h)),
            pl.BlockSpec((seq, 2 * LANES), lambda b, h, i: (b, h)),
            pl.BlockSpec((seq, 2 * LANES), lambda b, h, i: (b, h)),
        ],
        out_specs=pl.BlockSpec((tq, LANES), lambda b, h, i: (b * nq + i, h)),
        out_shape=jax.ShapeDtypeStruct((t, heads * hd), BF16),
        compiler_params=_params(3),
        name="fox_flash",
    )(qw, kw, vw)


def _pool_scan_kernel(lf_ref, tri_ref, ones_ref, within_ref, tot_ref):
    x = lf_ref[...]
    hi, lo = _split2(x.reshape(-1, x.shape[-1]))
    within_ref[...] = (_dot(hi, tri_ref[...]) + _dot(lo, tri_ref[...])).reshape(x.shape)
    tot_ref[...] = (_dot(hi, ones_ref[...]) + _dot(lo, ones_ref[...])).reshape(x.shape)


def _pool_scan(lf_cache, layer):
    _, n_pool, heads, page = lf_cache.shape
    pos = jnp.arange(page)
    tri = (pos[:, None] > pos[None, :]).astype(BF16)
    ones = jnp.ones((page, page), BF16)
    tm = _pick(n_pool, (512, 256, 128, 64, 32, 16, 8))
    blk = lambda i: (i, 0, 0)
    return pl.pallas_call(
        _pool_scan_kernel,
        grid=(n_pool // tm,),
        in_specs=[pl.BlockSpec((None, tm, heads, page), lambda i: (layer, i, 0, 0)),
                  pl.BlockSpec((page, page), lambda i: (0, 0)),
                  pl.BlockSpec((page, page), lambda i: (0, 0))],
        out_specs=[pl.BlockSpec((tm, heads, page), blk)] * 2,
        out_shape=[jax.ShapeDtypeStruct((n_pool, heads, page), F32)] * 2,
        compiler_params=_params(1),
        name="pool_scan",
    )(lf_cache, tri, ones)


def _fox_paged_kernel(pt_ref, q_ref, kn_ref, vn_ref, cn_ref, *refs, g, heads, hd, n_new, n_chunks):
    k_refs = refs[0:g]
    v_refs = refs[g:2 * g]
    w_refs = refs[2 * g:3 * g]
    t_refs = refs[3 * g:4 * g]
    o_ref, m_scr, l_scr, acc_scr, carry_scr, qbd_scr, rowc_scr = refs[4 * g:]
    c = pl.program_id(1)
    rows = heads * n_new
    w = heads * hd
    page = k_refs[0].shape[-1]
    own = (lax.broadcasted_iota(jnp.int32, (rows, w), 0) // n_new
           == lax.broadcasted_iota(jnp.int32, (rows, w), 1) // hd)

    @pl.when(c == 0)
    def _():
        q = q_ref[...].astype(F32)
        qbd = jnp.where(own, jnp.concatenate([q] * heads, axis=0), 0.0).astype(BF16)
        qbd_scr[...] = qbd
        cn = cn_ref[...]
        sel = (lax.broadcasted_iota(jnp.int32, (rows, LANES), 1)
               == lax.broadcasted_iota(jnp.int32, (rows, LANES), 0) // n_new)
        rowc = jnp.sum(jnp.where(sel, jnp.concatenate([cn] * heads, axis=0), 0.0), axis=-1, keepdims=True)
        rowc_scr[...] = jnp.broadcast_to(rowc, rowc_scr.shape)
        selb = jnp.where(sel, 1.0, 0.0).astype(BF16)
        c1, c2, c3 = _split3(cn)
        colc = (lax.dot_general(selb, c1, _NT, preferred_element_type=F32)
                + lax.dot_general(selb, c2, _NT, preferred_element_type=F32)
                + lax.dot_general(selb, c3, _NT, preferred_element_type=F32))
        s = lax.dot_general(qbd, kn_ref[...], _NT, preferred_element_type=F32) + rowc - colc
        tq = lax.broadcasted_iota(jnp.int32, (rows, n_new), 0) % n_new
        tk = lax.broadcasted_iota(jnp.int32, (rows, n_new), 1)
        s = jnp.where(tk <= tq, s, NEG_INF)
        m = jnp.max(s, axis=-1, keepdims=True)
        p = jnp.exp(s - m)
        m_scr[...] = jnp.broadcast_to(m, m_scr.shape)
        l_scr[...] = jnp.broadcast_to(jnp.sum(p, axis=-1, keepdims=True), l_scr.shape)
        acc_scr[...] = _dot(p.astype(BF16), vn_ref[...])
        carry_scr[...] = jnp.zeros_like(carry_scr)

    qbd = qbd_scr[...]
    rowc = rowc_scr[:, 0:1]
    carry = carry_scr[...]
    scores = []
    for j in range(g):
        kt = k_refs[j][...].reshape(w, page).astype(BF16)
        bias = w_refs[j][...] + carry
        carry = carry + t_refs[j][...]
        bias_rows = jnp.concatenate(
            [jnp.broadcast_to(bias[h:h + 1, :], (n_new, page)) for h in range(heads)], axis=0)
        scores.append(_dot(qbd, kt) + rowc + bias_rows)
    carry_scr[...] = carry
    s = jnp.concatenate(scores, axis=1)
    m_old = m_scr[:, 0:1]
    m_new = jnp.maximum(m_old, jnp.max(s, axis=-1, keepdims=True))
    p = jnp.exp(s - m_new)
    alpha = jnp.exp(m_old - m_new)
    l_scr[...] = jnp.broadcast_to(alpha * l_scr[:, 0:1] + jnp.sum(p, axis=-1, keepdims=True), l_scr.shape)
    m_scr[...] = jnp.broadcast_to(m_new, m_scr.shape)
    pb = p.astype(BF16)
    acc = acc_scr[...] * alpha
    for j in range(g):
        vt = v_refs[j][...].reshape(w, page).astype(BF16)
        acc = acc + lax.dot_general(pb[:, j * page:(j + 1) * page], vt, _NT, preferred_element_type=F32)
    acc_scr[...] = acc

    @pl.when(c == n_chunks - 1)
    def _():
        full = jnp.where(own, acc / l_scr[:, 0:1], 0.0)
        out = full[0:n_new, :]
        for h in range(1, heads):
            out = out + full[h * n_new:(h + 1) * n_new, :]
        o_ref[...] = out.astype(BF16)


def _fox_paged(qn, kb, vb, c_new, kt_cache, vt_cache, layer, within, tot, page_table, n_new):
    bsz, n_pages = page_table.shape
    _, n_pool, heads, hd, page = kt_cache.shape
    w = heads * hd
    g = _pick(n_pages, (16, 8, 4, 2, 1))
    n_chunks = n_pages // g
    rows = heads * n_new
    pt = page_table.reshape(-1)

    def page_of(b, c, pt_ref, j):
        return pt_ref[b * n_pages + n_pages - 1 - (c * g + j)]

    def cache_map(j):
        return lambda b, c, pt_ref: (layer, page_of(b, c, pt_ref, j), 0, 0, 0)

    def scan_map(j):
        return lambda b, c, pt_ref: (page_of(b, c, pt_ref, j), 0, 0)

    tok = lambda b, c, pt_ref: (b, 0)
    in_specs = [pl.BlockSpec((n_new, w), tok), pl.BlockSpec((n_new, w), tok), pl.BlockSpec((n_new, w), tok),
                pl.BlockSpec((n_new, LANES), tok)]
    in_specs += [pl.BlockSpec((None, None, heads, hd, page), cache_map(j)) for j in range(g)]
    in_specs += [pl.BlockSpec((None, None, heads, hd, page), cache_map(j)) for j in range(g)]
    in_specs += [pl.BlockSpec((None, heads, page), scan_map(j)) for j in range(g)]
    in_specs += [pl.BlockSpec((None, heads, page), scan_map(j)) for j in range(g)]
    kern = functools.partial(_fox_paged_kernel, g=g, heads=heads, hd=hd, n_new=n_new, n_chunks=n_chunks)
    grid_spec = pltpu.PrefetchScalarGridSpec(
        num_scalar_prefetch=1,
        grid=(bsz, n_chunks),
        in_specs=in_specs,
        out_specs=pl.BlockSpec((n_new, w), tok),
        scratch_shapes=[pltpu.VMEM((rows, LANES), F32), pltpu.VMEM((rows, LANES), F32),
                        pltpu.VMEM((rows, w), F32), pltpu.VMEM((heads, page), F32),
                        pltpu.VMEM((rows, w), BF16), pltpu.VMEM((rows, LANES), F32)],
    )
    return pl.pallas_call(
        kern,
        grid_spec=grid_spec,
        out_shape=jax.ShapeDtypeStruct((bsz * n_new, w), BF16),
        compiler_params=_params(2),
        name="fox_paged",
    )(pt, qn, kb, vb, c_new, *([kt_cache] * g), *([vt_cache] * g), *([within] * g), *([tot] * g))


def _merge_kernel(x_ref, ca_ref, ra_ref, fo_ref, g0_ref, g1_ref, g2_ref, wpw_ref, bpw_ref, wro_ref, wfo_ref,
                  wout_ref, o_ref):
    y_conv = _dot(ca_ref[...], wpw_ref[...]) + bpw_ref[...]
    y_ret = _dot(ra_ref[...], wro_ref[...])
    y_fox = _dot(fo_ref[...], wfo_ref[...])
    merged = (_sigmoid(g0_ref[...].astype(F32)) * y_conv + _sigmoid(g1_ref[...].astype(F32)) * y_ret
              + _sigmoid(g2_ref[...].astype(F32)) * y_fox)
    o_ref[...] = x_ref[...] + _dot(merged.astype(BF16), wout_ref[...])


def _merge(x, z, ca, ra, fo, wpw, bpw, wro, wfo, wout, gate_col0):
    t, d = x.shape
    cw = ca.shape[1]
    tm = _pick(t, (512, 256))
    gc = gate_col0 // d
    row = lambda i: (i, 0)
    full = lambda i: (0, 0)
    return pl.pallas_call(
        _merge_kernel,
        grid=(t // tm,),
        in_specs=[
            pl.BlockSpec((tm, d), row),
            pl.BlockSpec((tm, cw), row), pl.BlockSpec((tm, cw), row), pl.BlockSpec((tm, cw), row),
            pl.BlockSpec((tm, d), lambda i: (i, gc)),
            pl.BlockSpec((tm, d), lambda i: (i, gc + 1)),
            pl.BlockSpec((tm, d), lambda i: (i, gc + 2)),
            pl.BlockSpec((cw, d), full), pl.BlockSpec((1, d), full),
            pl.BlockSpec((cw, d), full), pl.BlockSpec((cw, d), full), pl.BlockSpec((d, d), full),
        ],
        out_specs=pl.BlockSpec((tm, d), row),
        out_shape=jax.ShapeDtypeStruct((t, d), F32),
        compiler_params=_params(1),
        name="merge",
    )(x, ca, ra, fo, z, z, z, wpw, bpw, wro, wfo, wout)


def _mem_kv_kernel(m_ref, g_ref, wk_ref, wv_ref, gk_ref, k_ref, v_ref, *, heads, hd):
    mb = _rms(m_ref[...], g_ref[...]).astype(BF16)
    k = _dot(mb, wk_ref[...])
    k_ref[...] = jnp.concatenate(
        [_rms(k[:, h * hd:(h + 1) * hd], gk_ref[...]) for h in range(heads)], axis=1)
    v_ref[...] = _dot(mb, wv_ref[...])


def _mem_kv(mem, g, wk, wv, gk, heads, hd):
    t, d = mem.shape
    w = heads * hd
    tm = _pick(t, (512, 256, 128))
    full = lambda i: (0, 0)
    return pl.pallas_call(
        functools.partial(_mem_kv_kernel, heads=heads, hd=hd),
        grid=(t // tm,),
        in_specs=[pl.BlockSpec((tm, d), lambda i: (i, 0)), pl.BlockSpec((1, d), full),
                  pl.BlockSpec((d, w), full), pl.BlockSpec((d, w), full), pl.BlockSpec((1, hd), full)],
        out_specs=[pl.BlockSpec((tm, w), lambda i: (i, 0))] * 2,
        out_shape=[jax.ShapeDtypeStruct((t, w), F32)] * 2,
        compiler_params=_params(1),
        name="mem_kv",
    )(mem, g, wk, wv, gk)


def _cross_kernel(x_ref, gx_ref, wq_ref, gq_ref, mk_ref, mv_ref, wo_ref, o_ref, *, heads, hd, by_head):
    x = x_ref[...]
    q = _dot(_rms(x, gx_ref[...]).astype(BF16), wq_ref[...])
    outs = []
    for h in range(heads):
        if by_head:
            mk = mk_ref[:, h, :].astype(BF16)
            mv = mv_ref[:, h, :].astype(BF16)
        else:
            mk = mk_ref[:, h * hd:(h + 1) * hd].astype(BF16)
            mv = mv_ref[:, h * hd:(h + 1) * hd].astype(BF16)
        qh = _rms(q[:, h * hd:(h + 1) * hd], gq_ref[...]).astype(BF16)
        s = lax.dot_general(qh, mk, _NT, preferred_element_type=F32) * (hd ** -0.5)
        p = jnp.exp(s - jnp.max(s, axis=-1, keepdims=True))
        p = p / jnp.sum(p, axis=-1, keepdims=True)
        outs.append(_dot(p.astype(BF16), mv))
    o = jnp.concatenate(outs, axis=1).astype(BF16)
    o_ref[...] = x + _dot(o, wo_ref[...])


def _cross(x, gx, wq, gq, mem_k, mem_v, layer, wo, bsz, seq, heads, hd):
    t, d = x.shape
    w = heads * hd
    tm = _pick(seq, (512, 256, 128, 64, 32, 16, 8))
    n = seq // tm
    full = lambda b, i: (0, 0)
    by_head = mem_k.ndim == 5
    if by_head:
        n_mem = mem_k.shape[2]
        mem_spec = pl.BlockSpec((None, None, n_mem, heads, hd), lambda b, i: (layer, b, 0, 0, 0))
    else:
        n_mem = mem_k.shape[1]
        mem_spec = pl.BlockSpec((None, n_mem, w), lambda b, i: (b, 0, 0))
    return pl.pallas_call(
        functools.partial(_cross_kernel, heads=heads, hd=hd, by_head=by_head),
        grid=(bsz, n),
        in_specs=[pl.BlockSpec((tm, d), lambda b, i: (b * n + i, 0)), pl.BlockSpec((1, d), full),
                  pl.BlockSpec((d, w), full), pl.BlockSpec((1, hd), full),
                  mem_spec, mem_spec,
                  pl.BlockSpec((w, d), full)],
        out_specs=pl.BlockSpec((tm, d), lambda b, i: (b * n + i, 0)),
        out_shape=jax.ShapeDtypeStruct((t, d), F32),
        compiler_params=_params(2),
        name="cross_attn",
    )(x, gx, wq, gq, mem_k, mem_v, wo)


def _ffn_kernel(x_ref, g_ref, w1_ref, w2_ref, o_ref, h_scr, acc_scr):
    k = pl.program_id(1)

    @pl.when(k == 0)
    def _():
        h_scr[...] = _rms(x_ref[...], g_ref[...]).astype(BF16)
        acc_scr[...] = jnp.zeros_like(acc_scr)

    a = jnp.maximum(_dot(h_scr[...], w1_ref[...]), 0.0)
    acc_scr[...] += _dot((a * a).astype(BF16), w2_ref[...])

    @pl.when(k == pl.num_programs(1) - 1)
    def _():
        o_ref[...] = x_ref[...] + acc_scr[...]


def _ffn(x, g, w1, w2):
    t, d = x.shape
    dff = w1.shape[1]
    tm = _pick(t, (512, 256))
    tf = 1024
    return pl.pallas_call(
        _ffn_kernel,
        grid=(t // tm, dff // tf),
        in_specs=[pl.BlockSpec((tm, d), lambda i, k: (i, 0)), pl.BlockSpec((1, d), lambda i, k: (0, 0)),
                  pl.BlockSpec((d, tf), lambda i, k: (0, k)), pl.BlockSpec((tf, d), lambda i, k: (k, 0))],
        out_specs=pl.BlockSpec((tm, d), lambda i, k: (i, 0)),
        out_shape=jax.ShapeDtypeStruct((t, d), F32),
        scratch_shapes=[pltpu.VMEM((tm, d), BF16), pltpu.VMEM((tm, d), F32)],
        compiler_params=_params(2),
        name="ffn",
    )(x, g, w1, w2)


def _layer(x, bsz, seq, conv_hist, ret_state0, ret_tables, mem_k, mem_v, layer, lw, dims, fox):
    d_conv, ret_heads, ret_dk, ret_dv, fox_heads, fox_hd, x_heads, x_hd = dims
    prompt = fox[0] == "prompt"
    z_dtype = BF16 if seq % BF16_ROWS == 0 else F32
    z, lf = _in_proj(x, lw["g_mix"], lw["w_main"], lw["b_main"], lw["w_f"], lw["b_f"], z_dtype)
    c_act, new_buf = _conv_branch(z, conv_hist, lw["conv_w"], lw["conv_b"], lw["conv_ln_g"], lw["conv_ln_b"],
                                  bsz, seq)
    ret_col0 = 2 * d_conv
    r_act, new_state = _ret_branch(z, ret_tables, lw["ret_gn"], ret_state0, bsz, seq, ret_col0)
    fox_col0 = ret_col0 + 2 * ret_heads * ret_dk + 2 * ret_heads * ret_dv
    if prompt:
        _, kbuf, vbuf = fox
        qw, kw, vw, kbuf, vbuf = _fox_prep_prompt(z, lf, lw["fox_g_q"], lw["fox_g_k"], kbuf, vbuf, layer,
                                                  bsz, seq, fox_col0, fox_heads, fox_hd)
        fo = _fox_flash(qw, kw, vw, bsz, seq, fox_heads, fox_hd)
        fox_out = (kbuf, vbuf)
    else:
        _, kt_cache, vt_cache, within, tot, page_table = fox
        qn, kf, kb, vf, vb, c = _fox_prep(z, lf, lw["fox_g_q"], lw["fox_g_k"], bsz, seq, fox_col0,
                                          fox_heads, fox_hd)
        fo = _fox_paged(qn, kb, vb, c, kt_cache, vt_cache, layer, within, tot, page_table, seq)
        fox_out = (kf, vf)
    gate_col0 = fox_col0 + 3 * fox_heads * fox_hd
    x = _merge(x, z, c_act, r_act, fo, lw["conv_w_pw"], lw["conv_b_pw"], lw["ret_w_o"], lw["fox_w_o"],
               lw["w_out"], gate_col0)
    x = _cross(x, lw["g_x"], lw["x_w_q"], lw["x_g_q"], mem_k, mem_v, layer, lw["x_w_o"], bsz, seq, x_heads, x_hd)
    x = _ffn(x, lw["g_ff"], lw["w_ff1"], lw["w_ff2"])
    return x, new_buf, new_state, fox_out, lf[:, :fox_heads]


def kernel(x_prompt, x_sample, cache_conv, state_ret, cache_fox_k, cache_fox_v, cache_fox_logf, cache_mem_k, cache_mem_v, page_table, mem_prompt, g_mix, w_in, b_in, conv_w, conv_b, conv_ln_g, conv_ln_b, conv_w_pw, conv_b_pw, ret_gn, ret_w_o, fox_g_q, fox_g_k, fox_w_o, w_out, g_x, g_mem, x_w_q, x_w_k, x_w_v, x_g_q, x_g_k, x_w_o, g_ff, w_ff1, w_ff2):
    bp, sp, d = x_prompt.shape
    bs, ss, _ = x_sample.shape
    depth = w_in.shape[0]
    width, d_conv = conv_w.shape[1], conv_w.shape[2]
    ret_heads, ret_dk, ret_dv = state_ret.shape[2], state_ret.shape[3], state_ret.shape[4]
    n_pool, page, fox_heads, fox_hd = cache_fox_k.shape[1:]
    n_mem, x_heads, x_hd = cache_mem_k.shape[2:]
    n_pages = page_table.shape[1]
    past = n_pages * page
    dims = (d_conv, ret_heads, ret_dk, ret_dv, fox_heads, fox_hd, x_heads, x_hd)
    fw = fox_heads * fox_hd
    forget_col = 2 * d_conv + 2 * ret_heads * ret_dk + 2 * ret_heads * ret_dv + 3 * fw

    tables_p = _ret_tables(sp, 0, ret_heads, ret_dk, ret_dv)
    tables_s = _ret_tables(ss, past, ret_heads, ret_dk, ret_dv)
    conv0 = jnp.zeros((bp, width - 1, d_conv), F32)
    ret0 = jnp.zeros((bp, ret_heads, ret_dk, ret_dv), F32)
    kt_cache = jnp.transpose(cache_fox_k, (0, 1, 3, 4, 2))
    vt_cache = jnp.transpose(cache_fox_v, (0, 1, 3, 4, 2))
    lf_cache = jnp.transpose(cache_fox_logf, (0, 1, 3, 2))
    kbuf = jnp.zeros((depth, bp, fw, sp), F32)
    vbuf = jnp.zeros((depth, bp, fw, sp), F32)

    xp = x_prompt.reshape(bp * sp, d)
    xs = x_sample.reshape(bs * ss, d)
    mem2d = mem_prompt.reshape(bp * n_mem, d)
    outs_p = [[] for _ in range(5)]
    outs_s = [[] for _ in range(5)]
    row = lambda a: a.reshape(1, -1)
    for l in range(depth):
        w_f = jnp.pad(w_in[l][:, forget_col:forget_col + fox_heads], ((0, 0), (0, LANES - fox_heads)))
        b_f = jnp.pad(b_in[l][forget_col:forget_col + fox_heads], (0, LANES - fox_heads))
        lw = {
            "g_mix": row(g_mix[l]),
            "w_main": jnp.concatenate([w_in[l][:, :forget_col], w_in[l][:, forget_col + fox_heads:]],
                                      axis=1).astype(BF16),
            "b_main": row(jnp.concatenate([b_in[l][:forget_col], b_in[l][forget_col + fox_heads:]])),
            "w_f": w_f.astype(BF16), "b_f": row(b_f),
            "conv_w": conv_w[l], "conv_b": row(conv_b[l]),
            "conv_ln_g": row(conv_ln_g[l]), "conv_ln_b": row(conv_ln_b[l]),
            "conv_w_pw": conv_w_pw[l].astype(BF16), "conv_b_pw": row(conv_b_pw[l]),
            "ret_gn": row(ret_gn[l]), "ret_w_o": ret_w_o[l].astype(BF16),
            "fox_g_q": row(jnp.tile(fox_g_q[l], fox_heads)), "fox_g_k": row(jnp.tile(fox_g_k[l], fox_heads)),
            "fox_w_o": fox_w_o[l].astype(BF16), "w_out": w_out[l].astype(BF16),
            "g_x": row(g_x[l]), "x_w_q": x_w_q[l].astype(BF16), "x_g_q": row(x_g_q[l]),
            "x_w_o": x_w_o[l].astype(BF16), "g_ff": row(g_ff[l]),
            "w_ff1": w_ff1[l].astype(BF16), "w_ff2": w_ff2[l].astype(BF16),
        }
        mk, mv = _mem_kv(mem2d, row(g_mem[l]), x_w_k[l].astype(BF16), x_w_v[l].astype(BF16), row(x_g_k[l]),
                         x_heads, x_hd)
        mk3 = mk.reshape(bp, n_mem, x_heads * x_hd)
        mv3 = mv.reshape(bp, n_mem, x_heads * x_hd)
        xp, c_b, r_s, (kbuf, vbuf), l_f = _layer(xp, bp, sp, conv0, ret0, tables_p, mk3, mv3, l, lw, dims,
                                                 ("prompt", kbuf, vbuf))
        for lst, val in zip(outs_p, (c_b, r_s, l_f.reshape(bp, sp, fox_heads),
                                     mk.reshape(bp, n_mem, x_heads, x_hd), mv.reshape(bp, n_mem, x_heads, x_hd))):
            lst.append(val)
        within, tot = _pool_scan(lf_cache, l)
        xs, c_b, r_s, (f_k, f_v), l_f = _layer(xs, bs, ss, cache_conv[l], state_ret[l], tables_s,
                                               cache_mem_k, cache_mem_v, l, lw, dims,
                                               ("sample", kt_cache, vt_cache, within, tot, page_table))
        for lst, val in zip(outs_s, (c_b, r_s, f_k.reshape(bs, ss, fox_heads, fox_hd),
                                     f_v.reshape(bs, ss, fox_heads, fox_hd), l_f.reshape(bs, ss, fox_heads))):
            lst.append(val)
    conv_p, ret_p, lf_p, mk_p, mv_p = [jnp.stack(o) for o in outs_p]
    fk_p = jnp.transpose(kbuf.reshape(depth, bp, fox_heads, fox_hd, sp), (0, 1, 4, 2, 3))
    fv_p = jnp.transpose(vbuf.reshape(depth, bp, fox_heads, fox_hd, sp), (0, 1, 4, 2, 3))
    return (xp.reshape(bp, sp, d), xs.reshape(bs, ss, d), conv_p, ret_p, fk_p, fv_p, lf_p, mk_p, mv_p,
            *[jnp.stack(o) for o in outs_s])
```

```python
import functools
import math

import jax
import jax.numpy as jnp
from jax import lax
from jax.experimental import pallas as pl
from jax.experimental.pallas import tpu as pltpu

F32 = jnp.float32
BF16 = jnp.bfloat16
EPS = 1e-6
ROPE_BASE = 10000.0
RET_CHUNK = 128
LOG2E = 1.4426950408889634
LANES = 128
SUBLANES = 8
BF16_ROWS = 16
VMEM_LIMIT = 56 * 1024 * 1024
NEG_INF = float("-inf")
FLASH_HEADS = 4

_NT = (((1,), (1,)), ((), ()))


def _params(n_grid):
    return pltpu.CompilerParams(dimension_semantics=("arbitrary",) * n_grid,
                                vmem_limit_bytes=VMEM_LIMIT)


def _split2(x):
    hi = x.astype(BF16)
    lo = (x - hi.astype(F32)).astype(BF16)
    return hi, lo


def _split3(x):
    hi = x.astype(BF16)
    r = x - hi.astype(F32)
    mid = r.astype(BF16)
    lo = (r - mid.astype(F32)).astype(BF16)
    return hi, mid, lo


def _dot(a, b):
    return jnp.dot(a, b, preferred_element_type=F32)


def _rms(x, g):
    return x * lax.rsqrt(jnp.mean(x * x, axis=-1, keepdims=True) + EPS) * g


def _log_sigmoid(x):
    return jnp.minimum(x, 0.0) - jnp.log1p(jnp.exp(-jnp.abs(x)))


def _sigmoid(x):
    return 1.0 / (1.0 + jnp.exp(-x))


def _pick(n, cands):
    for c in cands:
        if n % c == 0:
            return c
    return n


def _head_norm(x, g, gm, hd, two_piece=True):
    if two_piece:
        hi, lo = _split2(x * x)
        ms = (_dot(hi, gm) + _dot(lo, gm)) * (1.0 / hd)
    else:
        ms = _dot((x * x).astype(BF16), gm) * (1.0 / hd)
    return x * lax.rsqrt(ms + EPS) * g


def _cumsum_rows(x, tri):
    h1, h2, h3 = _split3(x)
    return _dot(tri, h1) + _dot(tri, h2) + _dot(tri, h3)


def _in_proj_kernel(x_ref, g_ref, w_ref, b_ref, wf_ref, bf_ref, z_ref, lf_ref, h_scr):
    @pl.when(pl.program_id(1) == 0)
    def _():
        hb = _rms(x_ref[...], g_ref[...]).astype(BF16)
        h_scr[...] = hb
        lf_ref[...] = _log_sigmoid(_dot(hb, wf_ref[...]) + bf_ref[...])

    z_ref[...] = (_dot(h_scr[...], w_ref[...]) + b_ref[...]).astype(z_ref.dtype)


def _in_proj(x, g, w, b, wf, bf, z_dtype):
    t, d = x.shape
    n = w.shape[1]
    tm = _pick(t, (1024, 512, 256))
    tn = _pick(n, (1792, 1024))
    return pl.pallas_call(
        _in_proj_kernel,
        grid=(t // tm, n // tn),
        in_specs=[
            pl.BlockSpec((tm, d), lambda i, j: (i, 0)),
            pl.BlockSpec((1, d), lambda i, j: (0, 0)),
            pl.BlockSpec((d, tn), lambda i, j: (0, j)),
            pl.BlockSpec((1, tn), lambda i, j: (0, j)),
            pl.BlockSpec((d, LANES), lambda i, j: (0, 0)),
            pl.BlockSpec((1, LANES), lambda i, j: (0, 0)),
        ],
        out_specs=[
            pl.BlockSpec((tm, tn), lambda i, j: (i, j)),
            pl.BlockSpec((tm, LANES), lambda i, j: (i, 0)),
        ],
        out_shape=[jax.ShapeDtypeStruct((t, n), z_dtype), jax.ShapeDtypeStruct((t, LANES), F32)],
        scratch_shapes=[pltpu.VMEM((tm, d), BF16)],
        compiler_params=_params(2),
        name="in_proj",
    )(x, g, w, b, wf, bf)


def _conv_kernel(a_ref, b_ref, hist_ref, w_ref, cb_ref, lg_ref, lb_ref, shift_ref, act_ref, buf_ref, ext_scr,
                 *, tt, n_tiles, width, sub, pad):
    i = pl.program_id(1)
    hist = width - 1
    off = pad - hist
    kwin = sub + pad

    @pl.when(i == 0)
    def _():
        ext_scr[0:pad, :] = jnp.zeros((pad, ext_scr.shape[1]), F32)
        ext_scr[off:pad, :] = hist_ref[0]

    ext_scr[pad:pad + tt, :] = a_ref[...].astype(F32) * _sigmoid(b_ref[...].astype(F32))
    for r0 in range(0, tt, sub):
        win = ext_scr[r0:r0 + kwin, :]
        hi, lo = _split2(win)
        shifted = _dot(shift_ref[...], hi) + _dot(shift_ref[...], lo)
        acc = None
        for j in range(width):
            r, base = (off + j) % SUBLANES, (off + j) // SUBLANES * SUBLANES
            if r == 0:
                rows = win[base:base + sub, :]
            else:
                rows = shifted[(r - 1) * kwin + base:(r - 1) * kwin + base + sub, :]
            term = rows * w_ref[j:j + 1, :]
            acc = term if acc is None else acc + term
        y = acc + cb_ref[...]
        yc = y - jnp.mean(y, axis=-1, keepdims=True)
        yn = yc * lax.rsqrt(jnp.mean(yc * yc, axis=-1, keepdims=True) + EPS) * lg_ref[...] + lb_ref[...]
        act_ref[r0:r0 + sub, :] = (yn * _sigmoid(yn)).astype(BF16)

    @pl.when(i == n_tiles - 1)
    def _():
        buf_ref[0] = ext_scr[tt + off:tt + pad, :]

    if n_tiles > 1:
        ext_scr[off:pad, :] = ext_scr[tt + off:tt + pad, :]


def _conv_branch(z, hist, w, cb, lg, lb, bsz, seq):
    c = w.shape[1]
    width = w.shape[0]
    tt = _pick(seq, (256, 128, 64, 32, 16, 8))
    n_tiles = seq // tt
    sub = min(tt, 128)
    pad = 32
    assert width - 1 <= pad and (n_tiles == 1 or tt >= width - 1)
    kwin = sub + pad
    src = jnp.arange(kwin)
    shift = jnp.concatenate(
        [((src[:, None] + r == src[None, :]) & (src[:, None] + r < kwin)) for r in range(1, SUBLANES)],
        axis=0).astype(BF16)
    kern = functools.partial(_conv_kernel, tt=tt, n_tiles=n_tiles, width=width, sub=sub, pad=pad)
    return pl.pallas_call(
        kern,
        grid=(bsz, n_tiles),
        in_specs=[
            pl.BlockSpec((tt, c), lambda b, i: (b * n_tiles + i, 0)),
            pl.BlockSpec((tt, c), lambda b, i: (b * n_tiles + i, 1)),
            pl.BlockSpec((1, width - 1, c), lambda b, i: (b, 0, 0)),
            pl.BlockSpec((width, c), lambda b, i: (0, 0)),
            pl.BlockSpec((1, c), lambda b, i: (0, 0)),
            pl.BlockSpec((1, c), lambda b, i: (0, 0)),
            pl.BlockSpec((1, c), lambda b, i: (0, 0)),
            pl.BlockSpec(((SUBLANES - 1) * kwin, kwin), lambda b, i: (0, 0)),
        ],
        out_specs=[
            pl.BlockSpec((tt, c), lambda b, i: (b * n_tiles + i, 0)),
            pl.BlockSpec((1, width - 1, c), lambda b, i: (b, 0, 0)),
        ],
        out_shape=[jax.ShapeDtypeStruct((bsz * seq, c), BF16),
                   jax.ShapeDtypeStruct((bsz, width - 1, c), F32)],
        scratch_shapes=[pltpu.VMEM((pad + tt, c), F32)],
        compiler_params=_params(2),
        name="conv_branch",
    )(z, z, hist, w, cb, lg, lb, shift)


def _ret_kernel(q_ref, k_ref, v_ref, rg_ref, cos_ref, sin_ref, dm_ref, qd_ref, kd_ref, cd_ref, gn_ref,
                s0_ref, act_ref, s_ref, *, heads, dk, dv):
    @pl.when(pl.program_id(1) == 0)
    def _():
        s_ref[...] = s0_ref[...]

    cos = cos_ref[...]
    sin = sin_ref[...]
    half = dk // 2

    def rot(x):
        lane = lax.broadcasted_iota(jnp.int32, x.shape, 1)
        upper = (lane % dk) >= half
        swapped = jnp.where(upper, pltpu.roll(x, half, 1), pltpu.roll(x, x.shape[1] - half, 1))
        return x * cos + swapped * sin

    q = rot(q_ref[...].astype(F32)) * (dk ** -0.5)
    k = rot(k_ref[...].astype(F32))
    qb = q.astype(BF16)
    kb = k.astype(BF16)
    kdt = (k * kd_ref[...]).T.astype(BF16)
    vb = v_ref[...].astype(BF16)
    outs = []
    for h in range(heads):
        qh = qb[:, h * dk:(h + 1) * dk]
        kh = kb[:, h * dk:(h + 1) * dk]
        vh = vb[:, h * dv:(h + 1) * dv]
        inner = lax.dot_general(qh, kh, _NT, preferred_element_type=F32) * dm_ref[h]
        s_old = s_ref[0, h]
        o = _dot(inner.astype(BF16), vh) + _dot(qh, s_old.astype(BF16)) * qd_ref[:, h * dv:(h + 1) * dv]
        s_ref[0, h] = s_old * cd_ref[h] + _dot(kdt[h * dk:(h + 1) * dk, :], vh)
        outs.append(_rms(o, gn_ref[:, h * dv:(h + 1) * dv]))
    o_all = jnp.concatenate(outs, axis=1)
    rg = rg_ref[...].astype(F32)
    act_ref[...] = (rg * _sigmoid(rg) * o_all).astype(BF16)


def _ret_tables(seq, pos0, heads, dk, dv):
    c = math.gcd(seq, RET_CHUNK)
    half = dk // 2
    inv_freq = jnp.exp(-math.log(ROPE_BASE) * jnp.arange(half, dtype=F32) / half)
    pos = (pos0 + jnp.arange(seq)).astype(F32)
    ang = pos[:, None] * inv_freq[None, :]
    cos = jnp.tile(jnp.concatenate([jnp.cos(ang), jnp.cos(ang)], axis=-1), (1, heads))
    sin = jnp.tile(jnp.concatenate([-jnp.sin(ang), jnp.sin(ang)], axis=-1), (1, heads))
    log_g = jnp.log1p(-jnp.exp2(-5.0 - jnp.arange(heads, dtype=F32)))
    idx = jnp.arange(c, dtype=F32)
    diff = idx[:, None] - idx[None, :]
    dmask = jnp.where(diff >= 0, jnp.exp(log_g[:, None, None] * jnp.maximum(diff, 0.0)), 0.0)
    q_dec = jnp.exp(log_g[:, None] * (idx[None, :] + 1.0))
    k_dec = jnp.exp(log_g[:, None] * (c - 1.0 - idx[None, :]))
    chunk_dec = jnp.exp(log_g * c)
    qd = jnp.repeat(q_dec.T, dv, axis=1)
    kd = jnp.repeat(k_dec.T, dk, axis=1)
    cd = jnp.broadcast_to(chunk_dec[:, None, None], (heads, 1, dv))
    return c, cos, sin, dmask, qd, kd, cd


def _ret_branch(z, tables, gn, state0, bsz, seq, col0):
    heads, dk, dv = state0.shape[1], state0.shape[2], state0.shape[3]
    c, cos, sin, dmask, qd, kd, cd = tables
    n = seq // c
    hk, hv = heads * dk, heads * dv
    qcol, kcol, vcol, gcol = col0 // hk, col0 // hk + 1, (col0 + 2 * hk) // hv, (col0 + 2 * hk) // hv + 1
    kern = functools.partial(_ret_kernel, heads=heads, dk=dk, dv=dv)
    row = lambda b, i: b * n + i
    return pl.pallas_call(
        kern,
        grid=(bsz, n),
        in_specs=[
            pl.BlockSpec((c, hk), lambda b, i: (row(b, i), qcol)),
            pl.BlockSpec((c, hk), lambda b, i: (row(b, i), kcol)),
            pl.BlockSpec((c, hv), lambda b, i: (row(b, i), vcol)),
            pl.BlockSpec((c, hv), lambda b, i: (row(b, i), gcol)),
            pl.BlockSpec((c, hk), lambda b, i: (i, 0)),
            pl.BlockSpec((c, hk), lambda b, i: (i, 0)),
            pl.BlockSpec((heads, c, c), lambda b, i: (0, 0, 0)),
            pl.BlockSpec((c, hv), lambda b, i: (0, 0)),
            pl.BlockSpec((c, hk), lambda b, i: (0, 0)),
            pl.BlockSpec((heads, 1, dv), lambda b, i: (0, 0, 0)),
            pl.BlockSpec((1, hv), lambda b, i: (0, 0)),
            pl.BlockSpec((1, heads, dk, dv), lambda b, i: (b, 0, 0, 0)),
        ],
        out_specs=[
            pl.BlockSpec((c, hv), lambda b, i: (row(b, i), 0)),
            pl.BlockSpec((1, heads, dk, dv), lambda b, i: (b, 0, 0, 0)),
        ],
        out_shape=[jax.ShapeDtypeStruct((bsz * seq, hv), BF16),
                   jax.ShapeDtypeStruct(state0.shape, F32)],
        compiler_params=_params(2),
        name="retention",
    )(z, z, z, z, cos, sin, dmask, qd, kd, cd, gn, state0)


def _fox_prep_kernel(fq_ref, fk_ref, fv_ref, lf_ref, gq_ref, gk_ref, gm_ref, tri_ref,
                     qn_ref, kf_ref, kb_ref, vf_ref, vb_ref, c_ref, carry, *, hd):
    @pl.when(pl.program_id(1) == 0)
    def _():
        carry[...] = jnp.zeros_like(carry)

    gm = gm_ref[...]
    qn_ref[...] = (_head_norm(fq_ref[...].astype(F32), gq_ref[...], gm, hd) * (hd ** -0.5)).astype(BF16)
    kn = _head_norm(fk_ref[...].astype(F32), gk_ref[...], gm, hd)
    kf_ref[...] = kn
    kb_ref[...] = kn.astype(BF16)
    v = fv_ref[...].astype(F32)
    vf_ref[...] = v
    vb_ref[...] = v.astype(BF16)
    cs = _cumsum_rows(lf_ref[...], tri_ref[...]) + carry[0:1, :]
    c_ref[...] = cs
    tm = cs.shape[0]
    carry[...] = jnp.broadcast_to(cs[tm - 1:tm, :], carry.shape)


def _fox_consts(heads, hd, tm):
    gm = jnp.kron(jnp.eye(heads, dtype=F32), jnp.ones((hd, hd), F32)).astype(BF16)
    tri = jnp.tril(jnp.ones((tm, tm), F32)).astype(BF16)
    return gm, tri


def _fox_prep(z, lf, gq, gk, bsz, seq, col0, heads, hd):
    t = bsz * seq
    w = heads * hd
    tm = _pick(seq, (512, 256, 128, 64, 32, 16, 8))
    n = seq // tm
    qcol = col0 // w
    gm, tri = _fox_consts(heads, hd, tm)
    row = lambda b, i: (b * n + i, 0)
    full = lambda b, i: (0, 0)
    return pl.pallas_call(
        functools.partial(_fox_prep_kernel, hd=hd),
        grid=(bsz, n),
        in_specs=[
            pl.BlockSpec((tm, w), lambda b, i: (b * n + i, qcol)),
            pl.BlockSpec((tm, w), lambda b, i: (b * n + i, qcol + 1)),
            pl.BlockSpec((tm, w), lambda b, i: (b * n + i, qcol + 2)),
            pl.BlockSpec((tm, LANES), row),
            pl.BlockSpec((1, w), full), pl.BlockSpec((1, w), full),
            pl.BlockSpec((w, w), full), pl.BlockSpec((tm, tm), full),
        ],
        out_specs=[pl.BlockSpec((tm, w), row)] * 5 + [pl.BlockSpec((tm, LANES), row)],
        out_shape=[jax.ShapeDtypeStruct((t, w), BF16), jax.ShapeDtypeStruct((t, w), F32),
                   jax.ShapeDtypeStruct((t, w), BF16), jax.ShapeDtypeStruct((t, w), F32),
                   jax.ShapeDtypeStruct((t, w), BF16), jax.ShapeDtypeStruct((t, LANES), F32)],
        scratch_shapes=[pltpu.VMEM((SUBLANES, LANES), F32)],
        compiler_params=_params(2),
        name="fox_prep",
    )(z, z, z, lf, gq, gk, gm, tri)


def _fox_prep_prompt_kernel(fq_ref, fk_ref, fv_ref, lf_ref, gq_ref, gk_ref, gm_ref, tri_ref, place_ref,
                            cq_ref, ck_ref, rows_ref, kbuf_ref, vbuf_ref,
                            q_ref, k_ref, v_ref, kt_ref, vt_ref, carry, *, hd):
    del kbuf_ref, vbuf_ref

    @pl.when(pl.program_id(1) == 0)
    def _():
        carry[...] = jnp.zeros_like(carry)

    gm = gm_ref[...]
    place = place_ref[...]
    qn = _head_norm(fq_ref[...].astype(F32), gq_ref[...], gm, hd, two_piece=False) * (LOG2E * hd ** -0.5)
    kn = _head_norm(fk_ref[...].astype(F32), gk_ref[...], gm, hd)
    v = fv_ref[...].astype(F32)
    cs = _cumsum_rows(lf_ref[...], tri_ref[...]) + carry[0:1, :]
    tm = cs.shape[0]
    carry[...] = jnp.broadcast_to(cs[tm - 1:tm, :], carry.shape)
    c1, c2, c3 = _split3(cs * LOG2E)
    q_ref[...] = (_dot(qn.astype(BF16), place) + _dot(c1, cq_ref[0]) + _dot(c2, cq_ref[1])
                  + _dot(c3, cq_ref[2]) + rows_ref[0:1, :]).astype(BF16)
    k_ref[...] = (_dot(kn.astype(BF16), place) + _dot(c1, ck_ref[0]) + _dot(c2, ck_ref[1])
                  + _dot(c3, ck_ref[2]) + rows_ref[1:2, :]).astype(BF16)
    v_ref[...] = (_dot(v.astype(BF16), place) + rows_ref[2:3, :]).astype(BF16)
    kt_ref[...] = kn.T
    vt_ref[...] = v.T


def _fox_prep_prompt(z, lf, gq, gk, kbuf, vbuf, layer, bsz, seq, col0, heads, hd):
    t = bsz * seq
    w = heads * hd
    wide = heads * LANES
    tm = _pick(seq, (512, 256, 128))
    n = seq // tm
    qcol = col0 // w
    gm, tri = _fox_consts(heads, hd, tm)
    src = jnp.arange(w)
    place = (((src // hd) * LANES + src % hd)[:, None] == jnp.arange(wide)[None, :]).astype(BF16)
    head = jnp.arange(LANES)[:, None]
    dst = jnp.arange(wide)[None, :]

    def lane_sel(offset):
        return (dst == head * LANES + hd + offset) & (head < heads)

    cq = jnp.stack([lane_sel(i) for i in range(3)]).astype(BF16)
    ck = -jnp.stack([lane_sel(3 + i) for i in range(3)]).astype(BF16)
    lane_in_head = jnp.arange(wide) % LANES
    rows = jnp.stack([((lane_in_head >= hd + 3) & (lane_in_head < hd + 6)).astype(F32),
                      ((lane_in_head >= hd) & (lane_in_head < hd + 3)).astype(F32),
                      (lane_in_head == hd).astype(F32)] + [jnp.zeros((wide,), F32)] * (SUBLANES - 3))
    row = lambda b, i: (b * n + i, 0)
    full = lambda b, i: (0, 0)
    full3 = lambda b, i: (0, 0, 0)
    tspec = pl.BlockSpec((None, None, w, tm), lambda b, i: (layer, b, 0, i))
    return pl.pallas_call(
        functools.partial(_fox_prep_prompt_kernel, hd=hd),
        grid=(bsz, n),
        in_specs=[
            pl.BlockSpec((tm, w), lambda b, i: (b * n + i, qcol)),
            pl.BlockSpec((tm, w), lambda b, i: (b * n + i, qcol + 1)),
            pl.BlockSpec((tm, w), lambda b, i: (b * n + i, qcol + 2)),
            pl.BlockSpec((tm, LANES), row),
            pl.BlockSpec((1, w), full), pl.BlockSpec((1, w), full),
            pl.BlockSpec((w, w), full), pl.BlockSpec((tm, tm), full),
            pl.BlockSpec((w, wide), full),
            pl.BlockSpec((3, LANES, wide), full3), pl.BlockSpec((3, LANES, wide), full3),
            pl.BlockSpec((SUBLANES, wide), full),
            pl.BlockSpec(memory_space=pl.ANY), pl.BlockSpec(memory_space=pl.ANY),
        ],
        out_specs=[pl.BlockSpec((tm, wide), row)] * 3 + [tspec, tspec],
        out_shape=[jax.ShapeDtypeStruct((t, wide), BF16)] * 3
        + [jax.ShapeDtypeStruct(kbuf.shape, F32), jax.ShapeDtypeStruct(vbuf.shape, F32)],
        input_output_aliases={12: 3, 13: 4},
        scratch_shapes=[pltpu.VMEM((SUBLANES, LANES), F32)],
        compiler_params=_params(2),
        name="fox_prep_prompt",
    )(z, z, z, lf, gq, gk, gm, tri, place, cq, ck, rows, kbuf, vbuf)


def _fox_flash_kernel(q_ref, k_ref, v_ref, o_ref, *, tq, hd, nh):
    i = pl.program_id(2)
    lane = lax.broadcasted_iota(jnp.int32, (tq, LANES), 1)

    def block(j, carry, masked):
        start = pl.multiple_of(j * tq, tq)
        out = []
        for e in range(nh):
            m_old, acc = carry[e]
            cols = slice(e * LANES, (e + 1) * LANES)
            s = lax.dot_general(q_ref[:, cols], k_ref[pl.ds(start, tq), cols], _NT, preferred_element_type=F32)
            if masked:
                r = lax.broadcasted_iota(jnp.int32, (tq, tq), 0)
                c = lax.broadcasted_iota(jnp.int32, (tq, tq), 1)
                s = jnp.where(c <= r, s, NEG_INF)
            m_new = jnp.maximum(m_old, jnp.max(s, axis=-1, keepdims=True))
            p = jnp.exp2(s - m_new)
            acc = acc * jnp.exp2(m_old - m_new) + _dot(p.astype(BF16), v_ref[pl.ds(start, tq), cols])
            out.append((m_new, acc))
        return tuple(out)

    init = ((jnp.full((tq, 1), NEG_INF, F32), jnp.zeros((tq, LANES), F32)),) * nh
    carry = lax.fori_loop(0, i, functools.partial(block, masked=False), init)
    carry = block(i, carry, True)
    outs = []
    for e in range(nh):
        acc = carry[e][1]
        denom = jnp.sum(jnp.where(lane == hd, acc, 0.0), axis=-1, keepdims=True)
        outs.append(acc / denom)
    for e in range(0, nh, 2):
        o_ref[:, e // 2 * LANES:(e // 2 + 1) * LANES] = jnp.where(
            lane < hd, outs[e], pltpu.roll(outs[e + 1], hd, 1)).astype(BF16)


def _fox_flash(qw, kw, vw, bsz, seq, heads, hd, nh):
    t = bsz * seq
    tq = _pick(seq, (512, 256, 128))
    nq = seq // tq
    kern = functools.partial(_fox_flash_kernel, tq=tq, hd=hd, nh=nh)
    return pl.pallas_call(
        kern,
        grid=(bsz, heads // nh, nq),
        in_specs=[
            pl.BlockSpec((tq, nh * LANES), lambda b, h, i: (b * nq + i, h)),
            pl.BlockSpec((seq, nh * LANES), lambda b, h, i: (b, h)),
            pl.BlockSpec((seq, nh * LANES), lambda b, h, i: (b, h)),
        ],
        out_specs=pl.BlockSpec((tq, nh * hd), lambda b, h, i: (b * nq + i, h)),
        out_shape=jax.ShapeDtypeStruct((t, heads * hd), BF16),
        compiler_params=_params(3),
        name="fox_flash",
    )(qw, kw, vw)


def _fox_paged_kernel(pt_ref, q_ref, kn_ref, vn_ref, cn_ref, tri_ref, ones_ref, *refs,
                      g, heads, hd, n_new, n_chunks, n_pages, layer):
    k_refs = refs[0:g]
    lf_refs = refs[g:2 * g]
    v_hbm = refs[2 * g]
    o_ref, m_scr, l_scr, acc_scr, carry_scr, qbd_scr, rowc_scr, v_buf, v_sem = refs[2 * g + 1:]
    b = pl.program_id(0)
    c = pl.program_id(1)
    rows = heads * n_new
    w = heads * hd
    page = k_refs[0].shape[-1]
    own = (lax.broadcasted_iota(jnp.int32, (rows, w), 0) // n_new
           == lax.broadcasted_iota(jnp.int32, (rows, w), 1) // hd)

    @pl.when(c == 0)
    def _():
        q = q_ref[...].astype(F32)
        qbd = jnp.where(own, jnp.concatenate([q] * heads, axis=0), 0.0).astype(BF16)
        qbd_scr[...] = qbd
        cn = cn_ref[...]
        sel = (lax.broadcasted_iota(jnp.int32, (rows, LANES), 1)
               == lax.broadcasted_iota(jnp.int32, (rows, LANES), 0) // n_new)
        rowc = jnp.sum(jnp.where(sel, jnp.concatenate([cn] * heads, axis=0), 0.0), axis=-1, keepdims=True)
        rowc_scr[...] = jnp.broadcast_to(rowc, rowc_scr.shape)
        selb = jnp.where(sel, 1.0, 0.0).astype(BF16)
        c1, c2, c3 = _split3(cn)
        colc = (lax.dot_general(selb, c1, _NT, preferred_element_type=F32)
                + lax.dot_general(selb, c2, _NT, preferred_element_type=F32)
                + lax.dot_general(selb, c3, _NT, preferred_element_type=F32))
        s = lax.dot_general(qbd, kn_ref[...], _NT, preferred_element_type=F32) + rowc - colc
        tq = lax.broadcasted_iota(jnp.int32, (rows, n_new), 0) % n_new
        tk = lax.broadcasted_iota(jnp.int32, (rows, n_new), 1)
        s = jnp.where(tk <= tq, s, NEG_INF)
        m = jnp.max(s, axis=-1, keepdims=True)
        p = jnp.exp(s - m)
        m_scr[...] = jnp.broadcast_to(m, m_scr.shape)
        l_scr[...] = jnp.broadcast_to(jnp.sum(p, axis=-1, keepdims=True), l_scr.shape)
        acc_scr[...] = _dot(p.astype(BF16), vn_ref[...])
        carry_scr[...] = jnp.zeros_like(carry_scr)

    hi, lo = _split2(jnp.concatenate([lf_refs[j][...] for j in range(g)], axis=0))
    within = _dot(hi, tri_ref[...]) + _dot(lo, tri_ref[...])
    tot = _dot(hi, ones_ref[...]) + _dot(lo, ones_ref[...])
    qbd = qbd_scr[...]
    rowc = rowc_scr[:, 0:1]
    carry = carry_scr[...]
    scores = []
    for j in range(g):
        kt = k_refs[j][...].reshape(w, page).astype(BF16)
        bias = within[j * heads:(j + 1) * heads, :] + carry
        carry = carry + tot[j * heads:(j + 1) * heads, :]
        bias_rows = jnp.concatenate(
            [jnp.broadcast_to(bias[h:h + 1, :], (n_new, page)) for h in range(heads)], axis=0)
        scores.append(_dot(qbd, kt) + rowc + bias_rows)
    carry_scr[...] = carry
    s = jnp.concatenate(scores, axis=1)
    m_old = m_scr[:, 0:1]
    m_new = jnp.maximum(m_old, jnp.max(s, axis=-1, keepdims=True))
    p = jnp.exp(s - m_new)

    @pl.when(jnp.max(p) > 0.0)
    def _():
        copies = []
        for j in range(g):
            page_id = pt_ref[b * n_pages + n_pages - 1 - (c * g + j)]
            copies.append(pltpu.make_async_copy(v_hbm.at[layer, page_id], v_buf.at[j], v_sem.at[j]))
            copies[j].start()
        alpha = jnp.exp(m_old - m_new)
        l_scr[...] = jnp.broadcast_to(alpha * l_scr[:, 0:1] + jnp.sum(p, axis=-1, keepdims=True), l_scr.shape)
        m_scr[...] = jnp.broadcast_to(m_new, m_scr.shape)
        pb = p.astype(BF16)
        acc = acc_scr[...] * alpha
        for j in range(g):
            copies[j].wait()
            vt = v_buf[j].reshape(w, page).astype(BF16)
            acc = acc + lax.dot_general(pb[:, j * page:(j + 1) * page], vt, _NT, preferred_element_type=F32)
        acc_scr[...] = acc

    @pl.when(c == n_chunks - 1)
    def _():
        full = jnp.where(own, acc_scr[...] / l_scr[:, 0:1], 0.0)
        out = full[0:n_new, :]
        for h in range(1, heads):
            out = out + full[h * n_new:(h + 1) * n_new, :]
        o_ref[...] = out.astype(BF16)


def _fox_paged(qn, kb, vb, c_new, kt_cache, vt_cache, lf_cache, layer, page_table, n_new):
    bsz, n_pages = page_table.shape
    _, n_pool, heads, hd, page = kt_cache.shape
    w = heads * hd
    g = _pick(n_pages, (16, 8, 4, 2, 1))
    n_chunks = n_pages // g
    rows = heads * n_new
    pt = page_table.reshape(-1)
    pos = jnp.arange(page)
    tri = (pos[:, None] > pos[None, :]).astype(BF16)
    ones = jnp.ones((page, page), BF16)

    def page_of(b, c, pt_ref, j):
        return pt_ref[b * n_pages + n_pages - 1 - (c * g + j)]

    tok = lambda b, c, pt_ref: (b, 0)
    const = lambda b, c, pt_ref: (0, 0)
    in_specs = [pl.BlockSpec((n_new, w), tok), pl.BlockSpec((n_new, w), tok), pl.BlockSpec((n_new, w), tok),
                pl.BlockSpec((n_new, LANES), tok),
                pl.BlockSpec((page, page), const), pl.BlockSpec((page, page), const)]
    in_specs += [pl.BlockSpec((None, None, heads, hd, page),
                              functools.partial(lambda b, c, pt_ref, j: (layer, page_of(b, c, pt_ref, j), 0, 0, 0), j=j))
                 for j in range(g)]
    in_specs += [pl.BlockSpec((None, None, heads, page),
                              functools.partial(lambda b, c, pt_ref, j: (layer, page_of(b, c, pt_ref, j), 0, 0), j=j))
                 for j in range(g)]
    in_specs += [pl.BlockSpec(memory_space=pl.ANY)]
    kern = functools.partial(_fox_paged_kernel, g=g, heads=heads, hd=hd, n_new=n_new, n_chunks=n_chunks,
                             n_pages=n_pages, layer=layer)
    grid_spec = pltpu.PrefetchScalarGridSpec(
        num_scalar_prefetch=1,
        grid=(bsz, n_chunks),
        in_specs=in_specs,
        out_specs=pl.BlockSpec((n_new, w), tok),
        scratch_shapes=[pltpu.VMEM((rows, LANES), F32), pltpu.VMEM((rows, LANES), F32),
                        pltpu.VMEM((rows, w), F32), pltpu.VMEM((heads, page), F32),
                        pltpu.VMEM((rows, w), BF16), pltpu.VMEM((rows, LANES), F32),
                        pltpu.VMEM((g, heads, hd, page), F32), pltpu.SemaphoreType.DMA((g,))],
    )
    return pl.pallas_call(
        kern,
        grid_spec=grid_spec,
        out_shape=jax.ShapeDtypeStruct((bsz * n_new, w), BF16),
        compiler_params=_params(2),
        name="fox_paged",
    )(pt, qn, kb, vb, c_new, tri, ones, *([kt_cache] * g), *([lf_cache] * g), vt_cache)


def _merge_kernel(x_ref, ca_ref, ra_ref, fo_ref, g0_ref, g1_ref, g2_ref, wpw_ref, bpw_ref, wro_ref, wfo_ref,
                  wout_ref, o_ref):
    y_conv = _dot(ca_ref[...], wpw_ref[...]) + bpw_ref[...]
    y_ret = _dot(ra_ref[...], wro_ref[...])
    y_fox = _dot(fo_ref[...], wfo_ref[...])
    merged = (_sigmoid(g0_ref[...].astype(F32)) * y_conv + _sigmoid(g1_ref[...].astype(F32)) * y_ret
              + _sigmoid(g2_ref[...].astype(F32)) * y_fox)
    o_ref[...] = x_ref[...] + _dot(merged.astype(BF16), wout_ref[...])


def _merge(x, z, ca, ra, fo, wpw, bpw, wro, wfo, wout, gate_col0):
    t, d = x.shape
    cw = ca.shape[1]
    tm = _pick(t, (512, 256))
    gc = gate_col0 // d
    row = lambda i: (i, 0)
    full = lambda i: (0, 0)
    return pl.pallas_call(
        _merge_kernel,
        grid=(t // tm,),
        in_specs=[
            pl.BlockSpec((tm, d), row),
            pl.BlockSpec((tm, cw), row), pl.BlockSpec((tm, cw), row), pl.BlockSpec((tm, cw), row),
            pl.BlockSpec((tm, d), lambda i: (i, gc)),
            pl.BlockSpec((tm, d), lambda i: (i, gc + 1)),
            pl.BlockSpec((tm, d), lambda i: (i, gc + 2)),
            pl.BlockSpec((cw, d), full), pl.BlockSpec((1, d), full),
            pl.BlockSpec((cw, d), full), pl.BlockSpec((cw, d), full), pl.BlockSpec((d, d), full),
        ],
        out_specs=pl.BlockSpec((tm, d), row),
        out_shape=jax.ShapeDtypeStruct((t, d), F32),
        compiler_params=_params(1),
        name="merge",
    )(x, ca, ra, fo, z, z, z, wpw, bpw, wro, wfo, wout)


def _mem_kv_kernel(m_ref, g_ref, wk_ref, wv_ref, gk_ref, k_ref, v_ref, *, heads, hd):
    mb = _rms(m_ref[...], g_ref[...]).astype(BF16)
    k = _dot(mb, wk_ref[...])
    k_ref[...] = jnp.concatenate(
        [_rms(k[:, h * hd:(h + 1) * hd], gk_ref[...]) for h in range(heads)], axis=1)
    v_ref[...] = _dot(mb, wv_ref[...])


def _mem_kv(mem, g, wk, wv, gk, heads, hd):
    t, d = mem.shape
    w = heads * hd
    tm = _pick(t, (512, 256, 128))
    full = lambda i: (0, 0)
    return pl.pallas_call(
        functools.partial(_mem_kv_kernel, heads=heads, hd=hd),
        grid=(t // tm,),
        in_specs=[pl.BlockSpec((tm, d), lambda i: (i, 0)), pl.BlockSpec((1, d), full),
                  pl.BlockSpec((d, w), full), pl.BlockSpec((d, w), full), pl.BlockSpec((1, hd), full)],
        out_specs=[pl.BlockSpec((tm, w), lambda i: (i, 0))] * 2,
        out_shape=[jax.ShapeDtypeStruct((t, w), F32)] * 2,
        compiler_params=_params(1),
        name="mem_kv",
    )(mem, g, wk, wv, gk)


def _cross_kernel(x_ref, gx_ref, wq_ref, gq_ref, mk_ref, mv_ref, wo_ref, o_ref, *, heads, hd, by_head):
    x = x_ref[...]
    q = _dot(_rms(x, gx_ref[...]).astype(BF16), wq_ref[...])
    outs = []
    for h in range(heads):
        if by_head:
            mk = mk_ref[:, h, :].astype(BF16)
            mv = mv_ref[:, h, :].astype(BF16)
        else:
            mk = mk_ref[:, h * hd:(h + 1) * hd].astype(BF16)
            mv = mv_ref[:, h * hd:(h + 1) * hd].astype(BF16)
        qh = _rms(q[:, h * hd:(h + 1) * hd], gq_ref[...]).astype(BF16)
        s = lax.dot_general(qh, mk, _NT, preferred_element_type=F32) * (hd ** -0.5)
        p = jnp.exp(s - jnp.max(s, axis=-1, keepdims=True))
        p = p / jnp.sum(p, axis=-1, keepdims=True)
        outs.append(_dot(p.astype(BF16), mv))
    o = jnp.concatenate(outs, axis=1).astype(BF16)
    o_ref[...] = x + _dot(o, wo_ref[...])


def _cross(x, gx, wq, gq, mem_k, mem_v, layer, wo, bsz, seq, heads, hd):
    t, d = x.shape
    w = heads * hd
    tm = _pick(seq, (512, 256, 128, 64, 32, 16, 8))
    n = seq // tm
    full = lambda b, i: (0, 0)
    by_head = mem_k.ndim == 5
    if by_head:
        n_mem = mem_k.shape[2]
        mem_spec = pl.BlockSpec((None, None, n_mem, heads, hd), lambda b, i: (layer, b, 0, 0, 0))
    else:
        n_mem = mem_k.shape[1]
        mem_spec = pl.BlockSpec((None, n_mem, w), lambda b, i: (b, 0, 0))
    return pl.pallas_call(
        functools.partial(_cross_kernel, heads=heads, hd=hd, by_head=by_head),
        grid=(bsz, n),
        in_specs=[pl.BlockSpec((tm, d), lambda b, i: (b * n + i, 0)), pl.BlockSpec((1, d), full),
                  pl.BlockSpec((d, w), full), pl.BlockSpec((1, hd), full),
                  mem_spec, mem_spec,
                  pl.BlockSpec((w, d), full)],
        out_specs=pl.BlockSpec((tm, d), lambda b, i: (b * n + i, 0)),
        out_shape=jax.ShapeDtypeStruct((t, d), F32),
        compiler_params=_params(2),
        name="cross_attn",
    )(x, gx, wq, gq, mem_k, mem_v, wo)


def _ffn_kernel(x_ref, g_ref, w1_ref, w2_ref, o_ref, h_scr, acc_scr):
    k = pl.program_id(1)

    @pl.when(k == 0)
    def _():
        h_scr[...] = _rms(x_ref[...], g_ref[...]).astype(BF16)
        acc_scr[...] = jnp.zeros_like(acc_scr)

    a = jnp.maximum(_dot(h_scr[...], w1_ref[...]), 0.0)
    acc_scr[...] += _dot((a * a).astype(BF16), w2_ref[...])

    @pl.when(k == pl.num_programs(1) - 1)
    def _():
        o_ref[...] = x_ref[...] + acc_scr[...]


def _ffn(x, g, w1, w2):
    t, d = x.shape
    dff = w1.shape[1]
    tm = _pick(t, (1024, 512, 256))
    tf = 1024
    return pl.pallas_call(
        _ffn_kernel,
        grid=(t // tm, dff // tf),
        in_specs=[pl.BlockSpec((tm, d), lambda i, k: (i, 0)), pl.BlockSpec((1, d), lambda i, k: (0, 0)),
                  pl.BlockSpec((d, tf), lambda i, k: (0, k)), pl.BlockSpec((tf, d), lambda i, k: (k, 0))],
        out_specs=pl.BlockSpec((tm, d), lambda i, k: (i, 0)),
        out_shape=jax.ShapeDtypeStruct((t, d), F32),
        scratch_shapes=[pltpu.VMEM((tm, d), BF16), pltpu.VMEM((tm, d), F32)],
        compiler_params=_params(2),
        name="ffn",
    )(x, g, w1, w2)


def _layer(x, bsz, seq, conv_hist, ret_state0, ret_tables, mem_k, mem_v, layer, lw, dims, fox):
    d_conv, ret_heads, ret_dk, ret_dv, fox_heads, fox_hd, x_heads, x_hd = dims
    prompt = fox[0] == "prompt"
    z_dtype = BF16 if seq % BF16_ROWS == 0 else F32
    z, lf = _in_proj(x, lw["g_mix"], lw["w_main"], lw["b_main"], lw["w_f"], lw["b_f"], z_dtype)
    c_act, new_buf = _conv_branch(z, conv_hist, lw["conv_w"], lw["conv_b"], lw["conv_ln_g"], lw["conv_ln_b"],
                                  bsz, seq)
    ret_col0 = 2 * d_conv
    r_act, new_state = _ret_branch(z, ret_tables, lw["ret_gn"], ret_state0, bsz, seq, ret_col0)
    fox_col0 = ret_col0 + 2 * ret_heads * ret_dk + 2 * ret_heads * ret_dv
    if prompt:
        _, kbuf, vbuf = fox
        qw, kw, vw, kbuf, vbuf = _fox_prep_prompt(z, lf, lw["fox_g_q"], lw["fox_g_k"], kbuf, vbuf, layer,
                                                  bsz, seq, fox_col0, fox_heads, fox_hd)
        fo = _fox_flash(qw, kw, vw, bsz, seq, fox_heads, fox_hd, FLASH_HEADS)
        fox_out = (kbuf, vbuf)
    else:
        _, kt_cache, vt_cache, lf_cache, page_table = fox
        qn, kf, kb, vf, vb, c = _fox_prep(z, lf, lw["fox_g_q"], lw["fox_g_k"], bsz, seq, fox_col0,
                                          fox_heads, fox_hd)
        fo = _fox_paged(qn, kb, vb, c, kt_cache, vt_cache, lf_cache, layer, page_table, seq)
        fox_out = (kf, vf)
    gate_col0 = fox_col0 + 3 * fox_heads * fox_hd
    x = _merge(x, z, c_act, r_act, fo, lw["conv_w_pw"], lw["conv_b_pw"], lw["ret_w_o"], lw["fox_w_o"],
               lw["w_out"], gate_col0)
    x = _cross(x, lw["g_x"], lw["x_w_q"], lw["x_g_q"], mem_k, mem_v, layer, lw["x_w_o"], bsz, seq, x_heads, x_hd)
    x = _ffn(x, lw["g_ff"], lw["w_ff1"], lw["w_ff2"])
    return x, new_buf, new_state, fox_out, lf[:, :fox_heads]


def kernel(x_prompt, x_sample, cache_conv, state_ret, cache_fox_k, cache_fox_v, cache_fox_logf, cache_mem_k, cache_mem_v, page_table, mem_prompt, g_mix, w_in, b_in, conv_w, conv_b, conv_ln_g, conv_ln_b, conv_w_pw, conv_b_pw, ret_gn, ret_w_o, fox_g_q, fox_g_k, fox_w_o, w_out, g_x, g_mem, x_w_q, x_w_k, x_w_v, x_g_q, x_g_k, x_w_o, g_ff, w_ff1, w_ff2):
    bp, sp, d = x_prompt.shape
    bs, ss, _ = x_sample.shape
    depth = w_in.shape[0]
    width, d_conv = conv_w.shape[1], conv_w.shape[2]
    ret_heads, ret_dk, ret_dv = state_ret.shape[2], state_ret.shape[3], state_ret.shape[4]
    n_pool, page, fox_heads, fox_hd = cache_fox_k.shape[1:]
    n_mem, x_heads, x_hd = cache_mem_k.shape[2:]
    n_pages = page_table.shape[1]
    past = n_pages * page
    dims = (d_conv, ret_heads, ret_dk, ret_dv, fox_heads, fox_hd, x_heads, x_hd)
    fw = fox_heads * fox_hd
    forget_col = 2 * d_conv + 2 * ret_heads * ret_dk + 2 * ret_heads * ret_dv + 3 * fw

    tables_p = _ret_tables(sp, 0, ret_heads, ret_dk, ret_dv)
    tables_s = _ret_tables(ss, past, ret_heads, ret_dk, ret_dv)
    conv0 = jnp.zeros((bp, width - 1, d_conv), F32)
    ret0 = jnp.zeros((bp, ret_heads, ret_dk, ret_dv), F32)
    kt_cache = jnp.transpose(cache_fox_k, (0, 1, 3, 4, 2))
    vt_cache = jnp.transpose(cache_fox_v, (0, 1, 3, 4, 2))
    lf_cache = jnp.transpose(cache_fox_logf, (0, 1, 3, 2))
    kbuf = jnp.zeros((depth, bp, fw, sp), F32)
    vbuf = jnp.zeros((depth, bp, fw, sp), F32)

    xp = x_prompt.reshape(bp * sp, d)
    xs = x_sample.reshape(bs * ss, d)
    mem2d = mem_prompt.reshape(bp * n_mem, d)
    outs_p = [[] for _ in range(5)]
    outs_s = [[] for _ in range(5)]
    row = lambda a: a.reshape(1, -1)
    for l in range(depth):
        w_f = jnp.pad(w_in[l][:, forget_col:forget_col + fox_heads], ((0, 0), (0, LANES - fox_heads)))
        b_f = jnp.pad(b_in[l][forget_col:forget_col + fox_heads], (0, LANES - fox_heads))
        lw = {
            "g_mix": row(g_mix[l]),
            "w_main": jnp.concatenate([w_in[l][:, :forget_col], w_in[l][:, forget_col + fox_heads:]],
                                      axis=1).astype(BF16),
            "b_main": row(jnp.concatenate([b_in[l][:forget_col], b_in[l][forget_col + fox_heads:]])),
            "w_f": w_f.astype(BF16), "b_f": row(b_f),
            "conv_w": conv_w[l], "conv_b": row(conv_b[l]),
            "conv_ln_g": row(conv_ln_g[l]), "conv_ln_b": row(conv_ln_b[l]),
            "conv_w_pw": conv_w_pw[l].astype(BF16), "conv_b_pw": row(conv_b_pw[l]),
            "ret_gn": row(ret_gn[l]), "ret_w_o": ret_w_o[l].astype(BF16),
            "fox_g_q": row(jnp.tile(fox_g_q[l], fox_heads)), "fox_g_k": row(jnp.tile(fox_g_k[l], fox_heads)),
            "fox_w_o": fox_w_o[l].astype(BF16), "w_out": w_out[l].astype(BF16),
            "g_x": row(g_x[l]), "x_w_q": x_w_q[l].astype(BF16), "x_g_q": row(x_g_q[l]),
            "x_w_o": x_w_o[l].astype(BF16), "g_ff": row(g_ff[l]),
            "w_ff1": w_ff1[l].astype(BF16), "w_ff2": w_ff2[l].astype(BF16),
        }
        mk, mv = _mem_kv(mem2d, row(g_mem[l]), x_w_k[l].astype(BF16), x_w_v[l].astype(BF16), row(x_g_k[l]),
                         x_heads, x_hd)
        mk3 = mk.reshape(bp, n_mem, x_heads * x_hd)
        mv3 = mv.reshape(bp, n_mem, x_heads * x_hd)
        xp, c_b, r_s, (kbuf, vbuf), l_f = _layer(xp, bp, sp, conv0, ret0, tables_p, mk3, mv3, l, lw, dims,
                                                 ("prompt", kbuf, vbuf))
        for lst, val in zip(outs_p, (c_b, r_s, l_f.reshape(bp, sp, fox_heads),
                                     mk.reshape(bp, n_mem, x_heads, x_hd), mv.reshape(bp, n_mem, x_heads, x_hd))):
            lst.append(val)
        xs, c_b, r_s, (f_k, f_v), l_f = _layer(xs, bs, ss, cache_conv[l], state_ret[l], tables_s,
                                               cache_mem_k, cache_mem_v, l, lw, dims,
                                               ("sample", kt_cache, vt_cache, lf_cache, page_table))
        for lst, val in zip(outs_s, (c_b, r_s, f_k.reshape(bs, ss, fox_heads, fox_hd),
                                     f_v.reshape(bs, ss, fox_heads, fox_hd), l_f.reshape(bs, ss, fox_heads))):
            lst.append(val)
    conv_p, ret_p, lf_p, mk_p, mv_p = [jnp.stack(o) for o in outs_p]
    fk_p = jnp.transpose(kbuf.reshape(depth, bp, fox_heads, fox_hd, sp), (0, 1, 4, 2, 3))
    fv_p = jnp.transpose(vbuf.reshape(depth, bp, fox_heads, fox_hd, sp), (0, 1, 4, 2, 3))
    return (xp.reshape(bp, sp, d), xs.reshape(bs, ss, d), conv_p, ret_p, fk_p, fv_p, lf_p, mk_p, mv_p,
            *[jnp.stack(o) for o in outs_s])
```

```python
import functools
import math

import jax
import jax.numpy as jnp
from jax import lax
from jax.experimental import pallas as pl
from jax.experimental.pallas import tpu as pltpu

F32 = jnp.float32
BF16 = jnp.bfloat16
EPS = 1e-6
ROPE_BASE = 10000.0
RET_CHUNK = 128
LOG2E = 1.4426950408889634
LANES = 128
SUBLANES = 8
BF16_ROWS = 16
VMEM_LIMIT = 56 * 1024 * 1024
NEG_INF = float("-inf")
FLASH_HEADS = 4

_NT = (((1,), (1,)), ((), ()))


def _params(n_grid):
    return pltpu.CompilerParams(dimension_semantics=("arbitrary",) * n_grid,
                                vmem_limit_bytes=VMEM_LIMIT)


def _split2(x):
    hi = x.astype(BF16)
    lo = (x - hi.astype(F32)).astype(BF16)
    return hi, lo


def _split3(x):
    hi = x.astype(BF16)
    r = x - hi.astype(F32)
    mid = r.astype(BF16)
    lo = (r - mid.astype(F32)).astype(BF16)
    return hi, mid, lo


def _dot(a, b):
    return jnp.dot(a, b, preferred_element_type=F32)


def _rms(x, g):
    return x * lax.rsqrt(jnp.mean(x * x, axis=-1, keepdims=True) + EPS) * g


def _log_sigmoid(x):
    return jnp.minimum(x, 0.0) - jnp.log1p(jnp.exp(-jnp.abs(x)))


def _sigmoid(x):
    return 1.0 / (1.0 + jnp.exp(-x))


def _pick(n, cands):
    for c in cands:
        if n % c == 0:
            return c
    return n


def _head_norm(x, g, gm, hd, two_piece=True):
    if two_piece:
        hi, lo = _split2(x * x)
        ms = (_dot(hi, gm) + _dot(lo, gm)) * (1.0 / hd)
    else:
        ms = _dot((x * x).astype(BF16), gm) * (1.0 / hd)
    return x * lax.rsqrt(ms + EPS) * g


def _cumsum_rows(x, tri):
    h1, h2, h3 = _split3(x)
    return _dot(tri, h1) + _dot(tri, h2) + _dot(tri, h3)


def _in_proj_kernel(x_ref, g_ref, w_ref, b_ref, wf_ref, bf_ref, z_ref, lf_ref, h_scr):
    @pl.when(pl.program_id(1) == 0)
    def _():
        hb = _rms(x_ref[...], g_ref[...]).astype(BF16)
        h_scr[...] = hb
        lf_ref[...] = _log_sigmoid(_dot(hb, wf_ref[...]) + bf_ref[...])

    z_ref[...] = (_dot(h_scr[...], w_ref[...]) + b_ref[...]).astype(z_ref.dtype)


def _in_proj(x, g, w, b, wf, bf, z_dtype):
    t, d = x.shape
    n = w.shape[1]
    tm = _pick(t, (1024, 512, 256))
    tn = _pick(n, (1792, 1024))
    return pl.pallas_call(
        _in_proj_kernel,
        grid=(t // tm, n // tn),
        in_specs=[
            pl.BlockSpec((tm, d), lambda i, j: (i, 0)),
            pl.BlockSpec((1, d), lambda i, j: (0, 0)),
            pl.BlockSpec((d, tn), lambda i, j: (0, j)),
            pl.BlockSpec((1, tn), lambda i, j: (0, j)),
            pl.BlockSpec((d, LANES), lambda i, j: (0, 0)),
            pl.BlockSpec((1, LANES), lambda i, j: (0, 0)),
        ],
        out_specs=[
            pl.BlockSpec((tm, tn), lambda i, j: (i, j)),
            pl.BlockSpec((tm, LANES), lambda i, j: (i, 0)),
        ],
        out_shape=[jax.ShapeDtypeStruct((t, n), z_dtype), jax.ShapeDtypeStruct((t, LANES), F32)],
        scratch_shapes=[pltpu.VMEM((tm, d), BF16)],
        compiler_params=_params(2),
        name="in_proj",
    )(x, g, w, b, wf, bf)


def _conv_kernel(a_ref, b_ref, hist_ref, w_ref, cb_ref, lg_ref, lb_ref, shift_ref, act_ref, buf_ref, ext_scr,
                 *, tt, n_tiles, width, sub, pad):
    i = pl.program_id(1)
    hist = width - 1
    off = pad - hist
    kwin = sub + pad

    @pl.when(i == 0)
    def _():
        ext_scr[0:pad, :] = jnp.zeros((pad, ext_scr.shape[1]), F32)
        ext_scr[off:pad, :] = hist_ref[0]

    ext_scr[pad:pad + tt, :] = a_ref[...].astype(F32) * _sigmoid(b_ref[...].astype(F32))
    for r0 in range(0, tt, sub):
        win = ext_scr[r0:r0 + kwin, :]
        hi, lo = _split2(win)
        shifted = _dot(shift_ref[...], hi) + _dot(shift_ref[...], lo)
        parts = []
        for cs in (slice(c0, c0 + LANES) for c0 in range(0, win.shape[1], LANES)):
            acc = None
            for j in range(width):
                r, base = (off + j) % SUBLANES, (off + j) // SUBLANES * SUBLANES
                if r == 0:
                    rows = win[base:base + sub, cs]
                else:
                    rows = shifted[(r - 1) * kwin + base:(r - 1) * kwin + base + sub, cs]
                term = rows * w_ref[j:j + 1, cs]
                acc = term if acc is None else acc + term
            parts.append(acc)
        y = jnp.concatenate(parts, axis=1) + cb_ref[...]
        yc = y - jnp.mean(y, axis=-1, keepdims=True)
        yn = yc * lax.rsqrt(jnp.mean(yc * yc, axis=-1, keepdims=True) + EPS) * lg_ref[...] + lb_ref[...]
        act_ref[r0:r0 + sub, :] = (yn * _sigmoid(yn)).astype(BF16)

    @pl.when(i == n_tiles - 1)
    def _():
        buf_ref[0] = ext_scr[tt + off:tt + pad, :]

    if n_tiles > 1:
        ext_scr[off:pad, :] = ext_scr[tt + off:tt + pad, :]


def _conv_branch(z, hist, w, cb, lg, lb, bsz, seq):
    c = w.shape[1]
    width = w.shape[0]
    tt = _pick(seq, (256, 128, 64, 32, 16, 8))
    n_tiles = seq // tt
    sub = min(tt, 128)
    pad = 32
    assert width - 1 <= pad and (n_tiles == 1 or tt >= width - 1)
    kwin = sub + pad
    src = jnp.arange(kwin)
    shift = jnp.concatenate(
        [((src[:, None] + r == src[None, :]) & (src[:, None] + r < kwin)) for r in range(1, SUBLANES)],
        axis=0).astype(BF16)
    kern = functools.partial(_conv_kernel, tt=tt, n_tiles=n_tiles, width=width, sub=sub, pad=pad)
    return pl.pallas_call(
        kern,
        grid=(bsz, n_tiles),
        in_specs=[
            pl.BlockSpec((tt, c), lambda b, i: (b * n_tiles + i, 0)),
            pl.BlockSpec((tt, c), lambda b, i: (b * n_tiles + i, 1)),
            pl.BlockSpec((1, width - 1, c), lambda b, i: (b, 0, 0)),
            pl.BlockSpec((width, c), lambda b, i: (0, 0)),
            pl.BlockSpec((1, c), lambda b, i: (0, 0)),
            pl.BlockSpec((1, c), lambda b, i: (0, 0)),
            pl.BlockSpec((1, c), lambda b, i: (0, 0)),
            pl.BlockSpec(((SUBLANES - 1) * kwin, kwin), lambda b, i: (0, 0)),
        ],
        out_specs=[
            pl.BlockSpec((tt, c), lambda b, i: (b * n_tiles + i, 0)),
            pl.BlockSpec((1, width - 1, c), lambda b, i: (b, 0, 0)),
        ],
        out_shape=[jax.ShapeDtypeStruct((bsz * seq, c), BF16),
                   jax.ShapeDtypeStruct((bsz, width - 1, c), F32)],
        scratch_shapes=[pltpu.VMEM((pad + tt, c), F32)],
        compiler_params=_params(2),
        name="conv_branch",
    )(z, z, hist, w, cb, lg, lb, shift)


def _ret_kernel(q_ref, k_ref, v_ref, rg_ref, cos_ref, sin_ref, dm_ref, qd_ref, kd_ref, cd_ref, gn_ref,
                s0_ref, act_ref, s_ref, *, heads, dk, dv, group):
    @pl.when(pl.program_id(1) == 0)
    def _():
        s_ref[...] = s0_ref[...]

    cos = cos_ref[...]
    sin = sin_ref[...]
    half = dk // 2

    def rot(x):
        lane = lax.broadcasted_iota(jnp.int32, x.shape, 1)
        upper = (lane % dk) >= half
        swapped = jnp.where(upper, pltpu.roll(x, half, 1), pltpu.roll(x, x.shape[1] - half, 1))
        return x * cos + swapped * sin

    for bb in range(group):
        q = rot(q_ref[bb].astype(F32)) * (dk ** -0.5)
        k = rot(k_ref[bb].astype(F32))
        qb = q.astype(BF16)
        kb = k.astype(BF16)
        kdt = (k * kd_ref[...]).T.astype(BF16)
        vb = v_ref[bb].astype(BF16)
        outs = []
        for h in range(heads):
            qh = qb[:, h * dk:(h + 1) * dk]
            kh = kb[:, h * dk:(h + 1) * dk]
            vh = vb[:, h * dv:(h + 1) * dv]
            inner = lax.dot_general(qh, kh, _NT, preferred_element_type=F32) * dm_ref[h]
            s_old = s_ref[bb, h]
            o = _dot(inner.astype(BF16), vh) + _dot(qh, s_old.astype(BF16)) * qd_ref[:, h * dv:(h + 1) * dv]
            s_ref[bb, h] = s_old * cd_ref[h] + _dot(kdt[h * dk:(h + 1) * dk, :], vh)
            outs.append(_rms(o, gn_ref[:, h * dv:(h + 1) * dv]))
        o_all = jnp.concatenate(outs, axis=1)
        rg = rg_ref[bb].astype(F32)
        act_ref[bb] = (rg * _sigmoid(rg) * o_all).astype(BF16)


def _ret_tables(seq, pos0, heads, dk, dv):
    c = math.gcd(seq, RET_CHUNK)
    half = dk // 2
    inv_freq = jnp.exp(-math.log(ROPE_BASE) * jnp.arange(half, dtype=F32) / half)
    pos = (pos0 + jnp.arange(seq)).astype(F32)
    ang = pos[:, None] * inv_freq[None, :]
    cos = jnp.tile(jnp.concatenate([jnp.cos(ang), jnp.cos(ang)], axis=-1), (1, heads))
    sin = jnp.tile(jnp.concatenate([-jnp.sin(ang), jnp.sin(ang)], axis=-1), (1, heads))
    log_g = jnp.log1p(-jnp.exp2(-5.0 - jnp.arange(heads, dtype=F32)))
    idx = jnp.arange(c, dtype=F32)
    diff = idx[:, None] - idx[None, :]
    dmask = jnp.where(diff >= 0, jnp.exp(log_g[:, None, None] * jnp.maximum(diff, 0.0)), 0.0)
    q_dec = jnp.exp(log_g[:, None] * (idx[None, :] + 1.0))
    k_dec = jnp.exp(log_g[:, None] * (c - 1.0 - idx[None, :]))
    chunk_dec = jnp.exp(log_g * c)
    qd = jnp.repeat(q_dec.T, dv, axis=1)
    kd = jnp.repeat(k_dec.T, dk, axis=1)
    cd = jnp.broadcast_to(chunk_dec[:, None, None], (heads, 1, dv))
    return c, cos, sin, dmask, qd, kd, cd


def _ret_branch(z, tables, gn, state0, bsz, seq, col0):
    heads, dk, dv = state0.shape[1], state0.shape[2], state0.shape[3]
    c, cos, sin, dmask, qd, kd, cd = tables
    n = seq // c
    hk, hv = heads * dk, heads * dv
    qcol, kcol, vcol, gcol = col0 // hk, col0 // hk + 1, (col0 + 2 * hk) // hv, (col0 + 2 * hk) // hv + 1
    group = _pick(bsz, (4, 2, 1))
    z3 = z.reshape(bsz, seq, z.shape[1])
    kern = functools.partial(_ret_kernel, heads=heads, dk=dk, dv=dv, group=group)
    act, state = pl.pallas_call(
        kern,
        grid=(bsz // group, n),
        in_specs=[
            pl.BlockSpec((group, c, hk), lambda g, i: (g, i, qcol)),
            pl.BlockSpec((group, c, hk), lambda g, i: (g, i, kcol)),
            pl.BlockSpec((group, c, hv), lambda g, i: (g, i, vcol)),
            pl.BlockSpec((group, c, hv), lambda g, i: (g, i, gcol)),
            pl.BlockSpec((c, hk), lambda g, i: (i, 0)),
            pl.BlockSpec((c, hk), lambda g, i: (i, 0)),
            pl.BlockSpec((heads, c, c), lambda g, i: (0, 0, 0)),
            pl.BlockSpec((c, hv), lambda g, i: (0, 0)),
            pl.BlockSpec((c, hk), lambda g, i: (0, 0)),
            pl.BlockSpec((heads, 1, dv), lambda g, i: (0, 0, 0)),
            pl.BlockSpec((1, hv), lambda g, i: (0, 0)),
            pl.BlockSpec((group, heads, dk, dv), lambda g, i: (g, 0, 0, 0)),
        ],
        out_specs=[
            pl.BlockSpec((group, c, hv), lambda g, i: (g, i, 0)),
            pl.BlockSpec((group, heads, dk, dv), lambda g, i: (g, 0, 0, 0)),
        ],
        out_shape=[jax.ShapeDtypeStruct((bsz, seq, hv), BF16),
                   jax.ShapeDtypeStruct(state0.shape, F32)],
        compiler_params=_params(2),
        name="retention",
    )(z3, z3, z3, z3, cos, sin, dmask, qd, kd, cd, gn, state0)
    return act.reshape(bsz * seq, hv), state


def _fox_prep_kernel(fq_ref, fk_ref, fv_ref, lf_ref, gq_ref, gk_ref, gm_ref, tri_ref,
                     qn_ref, kf_ref, kb_ref, vf_ref, vb_ref, c_ref, carry, *, hd):
    @pl.when(pl.program_id(1) == 0)
    def _():
        carry[...] = jnp.zeros_like(carry)

    gm = gm_ref[...]
    qn_ref[...] = (_head_norm(fq_ref[...].astype(F32), gq_ref[...], gm, hd) * (hd ** -0.5)).astype(BF16)
    kn = _head_norm(fk_ref[...].astype(F32), gk_ref[...], gm, hd)
    kf_ref[...] = kn
    kb_ref[...] = kn.astype(BF16)
    v = fv_ref[...].astype(F32)
    vf_ref[...] = v
    vb_ref[...] = v.astype(BF16)
    cs = _cumsum_rows(lf_ref[...], tri_ref[...]) + carry[0:1, :]
    c_ref[...] = cs
    tm = cs.shape[0]
    carry[...] = jnp.broadcast_to(cs[tm - 1:tm, :], carry.shape)


def _fox_consts(heads, hd, tm):
    gm = jnp.kron(jnp.eye(heads, dtype=F32), jnp.ones((hd, hd), F32)).astype(BF16)
    tri = jnp.tril(jnp.ones((tm, tm), F32)).astype(BF16)
    return gm, tri


def _fox_prep(z, lf, gq, gk, bsz, seq, col0, heads, hd):
    t = bsz * seq
    w = heads * hd
    tm = _pick(seq, (512, 256, 128, 64, 32, 16, 8))
    n = seq // tm
    qcol = col0 // w
    gm, tri = _fox_consts(heads, hd, tm)
    row = lambda b, i: (b * n + i, 0)
    full = lambda b, i: (0, 0)
    return pl.pallas_call(
        functools.partial(_fox_prep_kernel, hd=hd),
        grid=(bsz, n),
        in_specs=[
            pl.BlockSpec((tm, w), lambda b, i: (b * n + i, qcol)),
            pl.BlockSpec((tm, w), lambda b, i: (b * n + i, qcol + 1)),
            pl.BlockSpec((tm, w), lambda b, i: (b * n + i, qcol + 2)),
            pl.BlockSpec((tm, LANES), row),
            pl.BlockSpec((1, w), full), pl.BlockSpec((1, w), full),
            pl.BlockSpec((w, w), full), pl.BlockSpec((tm, tm), full),
        ],
        out_specs=[pl.BlockSpec((tm, w), row)] * 5 + [pl.BlockSpec((tm, LANES), row)],
        out_shape=[jax.ShapeDtypeStruct((t, w), BF16), jax.ShapeDtypeStruct((t, w), F32),
                   jax.ShapeDtypeStruct((t, w), BF16), jax.ShapeDtypeStruct((t, w), F32),
                   jax.ShapeDtypeStruct((t, w), BF16), jax.ShapeDtypeStruct((t, LANES), F32)],
        scratch_shapes=[pltpu.VMEM((SUBLANES, LANES), F32)],
        compiler_params=_params(2),
        name="fox_prep",
    )(z, z, z, lf, gq, gk, gm, tri)


def _fox_prep_prompt_kernel(fq_ref, fk_ref, fv_ref, lf_ref, gq_ref, gk_ref, gm_ref, tri_ref, place_ref,
                            cq_ref, ck_ref, rows_ref, kbuf_ref, vbuf_ref,
                            q_ref, k_ref, v_ref, kt_ref, vt_ref, carry, *, hd):
    del kbuf_ref, vbuf_ref

    @pl.when(pl.program_id(1) == 0)
    def _():
        carry[...] = jnp.zeros_like(carry)

    gm = gm_ref[...]
    place = place_ref[...]
    qn = _head_norm(fq_ref[...].astype(F32), gq_ref[...], gm, hd, two_piece=False) * (LOG2E * hd ** -0.5)
    kn = _head_norm(fk_ref[...].astype(F32), gk_ref[...], gm, hd)
    v = fv_ref[...].astype(F32)
    cs = _cumsum_rows(lf_ref[...], tri_ref[...]) + carry[0:1, :]
    tm = cs.shape[0]
    carry[...] = jnp.broadcast_to(cs[tm - 1:tm, :], carry.shape)
    c1, c2, c3 = _split3(cs * LOG2E)
    q_ref[...] = (_dot(qn.astype(BF16), place) + _dot(c1, cq_ref[0]) + _dot(c2, cq_ref[1])
                  + _dot(c3, cq_ref[2]) + rows_ref[0:1, :]).astype(BF16)
    k_ref[...] = (_dot(kn.astype(BF16), place) + _dot(c1, ck_ref[0]) + _dot(c2, ck_ref[1])
                  + _dot(c3, ck_ref[2]) + rows_ref[1:2, :]).astype(BF16)
    v_ref[...] = (_dot(v.astype(BF16), place) + rows_ref[2:3, :]).astype(BF16)
    kt_ref[...] = kn.T
    vt_ref[...] = v.T


def _fox_prep_prompt(z, lf, gq, gk, kbuf, vbuf, layer, bsz, seq, col0, heads, hd):
    t = bsz * seq
    w = heads * hd
    wide = heads * LANES
    tm = _pick(seq, (512, 256, 128))
    n = seq // tm
    qcol = col0 // w
    gm, tri = _fox_consts(heads, hd, tm)
    src = jnp.arange(w)
    place = (((src // hd) * LANES + src % hd)[:, None] == jnp.arange(wide)[None, :]).astype(BF16)
    head = jnp.arange(LANES)[:, None]
    dst = jnp.arange(wide)[None, :]

    def lane_sel(offset):
        return (dst == head * LANES + hd + offset) & (head < heads)

    cq = jnp.stack([lane_sel(i) for i in range(3)]).astype(BF16)
    ck = -jnp.stack([lane_sel(3 + i) for i in range(3)]).astype(BF16)
    lane_in_head = jnp.arange(wide) % LANES
    rows = jnp.stack([((lane_in_head >= hd + 3) & (lane_in_head < hd + 6)).astype(F32),
                      ((lane_in_head >= hd) & (lane_in_head < hd + 3)).astype(F32),
                      (lane_in_head == hd).astype(F32)] + [jnp.zeros((wide,), F32)] * (SUBLANES - 3))
    row = lambda b, i: (b * n + i, 0)
    full = lambda b, i: (0, 0)
    full3 = lambda b, i: (0, 0, 0)
    tspec = pl.BlockSpec((None, None, w, tm), lambda b, i: (layer, b, 0, i))
    return pl.pallas_call(
        functools.partial(_fox_prep_prompt_kernel, hd=hd),
        grid=(bsz, n),
        in_specs=[
            pl.BlockSpec((tm, w), lambda b, i: (b * n + i, qcol)),
            pl.BlockSpec((tm, w), lambda b, i: (b * n + i, qcol + 1)),
            pl.BlockSpec((tm, w), lambda b, i: (b * n + i, qcol + 2)),
            pl.BlockSpec((tm, LANES), row),
            pl.BlockSpec((1, w), full), pl.BlockSpec((1, w), full),
            pl.BlockSpec((w, w), full), pl.BlockSpec((tm, tm), full),
            pl.BlockSpec((w, wide), full),
            pl.BlockSpec((3, LANES, wide), full3), pl.BlockSpec((3, LANES, wide), full3),
            pl.BlockSpec((SUBLANES, wide), full),
            pl.BlockSpec(memory_space=pl.ANY), pl.BlockSpec(memory_space=pl.ANY),
        ],
        out_specs=[pl.BlockSpec((tm, wide), row)] * 3 + [tspec, tspec],
        out_shape=[jax.ShapeDtypeStruct((t, wide), BF16)] * 3
        + [jax.ShapeDtypeStruct(kbuf.shape, F32), jax.ShapeDtypeStruct(vbuf.shape, F32)],
        input_output_aliases={12: 3, 13: 4},
        scratch_shapes=[pltpu.VMEM((SUBLANES, LANES), F32)],
        compiler_params=_params(2),
        name="fox_prep_prompt",
    )(z, z, z, lf, gq, gk, gm, tri, place, cq, ck, rows, kbuf, vbuf)


def _fox_flash_kernel(q_ref, k_ref, v_ref, o_ref, *, tq, tk, hd, nh):
    i = pl.program_id(2)
    r = tq // tk
    lane = lax.broadcasted_iota(jnp.int32, (tq, LANES), 1)

    def update(m_old, acc, s, v_blk):
        m_new = jnp.maximum(m_old, jnp.max(s, axis=-1, keepdims=True))
        p = jnp.exp2(s - m_new)
        return m_new, acc * jnp.exp2(m_old - m_new) + _dot(p.astype(BF16), v_blk)

    def full_block(j, carry):
        start = pl.multiple_of(j * tk, tk)
        out = []
        for e in range(nh):
            cols = slice(e * LANES, (e + 1) * LANES)
            s = lax.dot_general(q_ref[:, cols], k_ref[pl.ds(start, tk), cols], _NT, preferred_element_type=F32)
            out.append(update(*carry[e], s, v_ref[pl.ds(start, tk), cols]))
        return tuple(out)

    init = ((jnp.full((tq, 1), NEG_INF, F32), jnp.zeros((tq, LANES), F32)),) * nh
    carry = list(lax.fori_loop(0, i * r, full_block, init))
    visible = (lax.broadcasted_iota(jnp.int32, (tk, tk), 1) <= lax.broadcasted_iota(jnp.int32, (tk, tk), 0))
    for d in range(r):
        start = pl.multiple_of((i * r + d) * tk, tk)
        lo = d * tk
        for e in range(nh):
            cols = slice(e * LANES, (e + 1) * LANES)
            m_old, acc = carry[e]
            s = lax.dot_general(q_ref[lo:, cols], k_ref[pl.ds(start, tk), cols], _NT, preferred_element_type=F32)
            top = jnp.where(visible, s[:tk], NEG_INF)
            s = top if tq - lo == tk else jnp.concatenate([top, s[tk:]], axis=0)
            m_new, acc_new = update(m_old[lo:], acc[lo:], s, v_ref[pl.ds(start, tk), cols])
            if lo:
                m_new = jnp.concatenate([m_old[:lo], m_new], axis=0)
                acc_new = jnp.concatenate([acc[:lo], acc_new], axis=0)
            carry[e] = (m_new, acc_new)
    outs = []
    for e in range(nh):
        acc = carry[e][1]
        denom = jnp.sum(jnp.where(lane == hd, acc, 0.0), axis=-1, keepdims=True)
        outs.append(acc / denom)
    for e in range(0, nh, 2):
        o_ref[:, e // 2 * LANES:(e // 2 + 1) * LANES] = jnp.where(
            lane < hd, outs[e], pltpu.roll(outs[e + 1], hd, 1)).astype(BF16)


def _fox_flash(qw, kw, vw, bsz, seq, heads, hd, nh):
    t = bsz * seq
    tk = _pick(seq, (512, 256, 128))
    tq = _pick(seq, (2 * tk, tk))
    nq = seq // tq
    kern = functools.partial(_fox_flash_kernel, tq=tq, tk=tk, hd=hd, nh=nh)
    return pl.pallas_call(
        kern,
        grid=(bsz, heads // nh, nq),
        in_specs=[
            pl.BlockSpec((tq, nh * LANES), lambda b, h, i: (b * nq + i, h)),
            pl.BlockSpec((seq, nh * LANES), lambda b, h, i: (b, h)),
            pl.BlockSpec((seq, nh * LANES), lambda b, h, i: (b, h)),
        ],
        out_specs=pl.BlockSpec((tq, nh * hd), lambda b, h, i: (b * nq + i, h)),
        out_shape=jax.ShapeDtypeStruct((t, heads * hd), BF16),
        compiler_params=_params(3),
        name="fox_flash",
    )(qw, kw, vw)


def _fox_paged_kernel(pt_ref, q_ref, kn_ref, vn_ref, cn_ref, tri_ref, ones_ref, *refs,
                      g, heads, hd, n_new, n_chunks, n_pages, layer):
    k_refs = refs[0:g]
    lf_refs = refs[g:2 * g]
    v_hbm = refs[2 * g]
    o_ref, m_scr, l_scr, acc_scr, carry_scr, qbd_scr, rowc_scr, v_buf, v_sem = refs[2 * g + 1:]
    b = pl.program_id(0)
    c = pl.program_id(1)
    rows = heads * n_new
    w = heads * hd
    page = k_refs[0].shape[-1]
    own = (lax.broadcasted_iota(jnp.int32, (rows, w), 0) // n_new
           == lax.broadcasted_iota(jnp.int32, (rows, w), 1) // hd)

    @pl.when(c == 0)
    def _():
        q = q_ref[...].astype(F32)
        qbd = jnp.where(own, jnp.concatenate([q] * heads, axis=0), 0.0).astype(BF16)
        qbd_scr[...] = qbd
        cn = cn_ref[...]
        sel = (lax.broadcasted_iota(jnp.int32, (rows, LANES), 1)
               == lax.broadcasted_iota(jnp.int32, (rows, LANES), 0) // n_new)
        rowc = jnp.sum(jnp.where(sel, jnp.concatenate([cn] * heads, axis=0), 0.0), axis=-1, keepdims=True)
        rowc_scr[...] = jnp.broadcast_to(rowc, rowc_scr.shape)
        selb = jnp.where(sel, 1.0, 0.0).astype(BF16)
        c1, c2, c3 = _split3(cn)
        colc = (lax.dot_general(selb, c1, _NT, preferred_element_type=F32)
                + lax.dot_general(selb, c2, _NT, preferred_element_type=F32)
                + lax.dot_general(selb, c3, _NT, preferred_element_type=F32))
        s = lax.dot_general(qbd, kn_ref[...], _NT, preferred_element_type=F32) + rowc - colc
        tq = lax.broadcasted_iota(jnp.int32, (rows, n_new), 0) % n_new
        tk = lax.broadcasted_iota(jnp.int32, (rows, n_new), 1)
        s = jnp.where(tk <= tq, s, NEG_INF)
        m = jnp.max(s, axis=-1, keepdims=True)
        p = jnp.exp(s - m)
        m_scr[...] = jnp.broadcast_to(m, m_scr.shape)
        l_scr[...] = jnp.broadcast_to(jnp.sum(p, axis=-1, keepdims=True), l_scr.shape)
        acc_scr[...] = _dot(p.astype(BF16), vn_ref[...])
        carry_scr[...] = jnp.zeros_like(carry_scr)

    hi, lo = _split2(jnp.concatenate([lf_refs[j][...] for j in range(g)], axis=0))
    within = _dot(hi, tri_ref[...]) + _dot(lo, tri_ref[...])
    tot = _dot(hi, ones_ref[...]) + _dot(lo, ones_ref[...])
    qbd = qbd_scr[...]
    rowc = rowc_scr[:, 0:1]
    carry = carry_scr[...]
    scores = []
    for j in range(g):
        kt = k_refs[j][...].reshape(w, page).astype(BF16)
        bias = within[j * heads:(j + 1) * heads, :] + carry
        carry = carry + tot[j * heads:(j + 1) * heads, :]
        bias_rows = jnp.concatenate(
            [jnp.broadcast_to(bias[h:h + 1, :], (n_new, page)) for h in range(heads)], axis=0)
        scores.append(_dot(qbd, kt) + rowc + bias_rows)
    carry_scr[...] = carry
    s = jnp.concatenate(scores, axis=1)
    m_old = m_scr[:, 0:1]
    m_new = jnp.maximum(m_old, jnp.max(s, axis=-1, keepdims=True))
    p = jnp.exp(s - m_new)

    @pl.when(jnp.max(p) > 0.0)
    def _():
        copies = []
        for j in range(g):
            page_id = pt_ref[b * n_pages + n_pages - 1 - (c * g + j)]
            copies.append(pltpu.make_async_copy(v_hbm.at[layer, page_id], v_buf.at[j], v_sem.at[j]))
            copies[j].start()
        alpha = jnp.exp(m_old - m_new)
        l_scr[...] = jnp.broadcast_to(alpha * l_scr[:, 0:1] + jnp.sum(p, axis=-1, keepdims=True), l_scr.shape)
        m_scr[...] = jnp.broadcast_to(m_new, m_scr.shape)
        pb = p.astype(BF16)
        acc = acc_scr[...] * alpha
        for j in range(g):
            copies[j].wait()
            vt = v_buf[j].reshape(w, page).astype(BF16)
            acc = acc + lax.dot_general(pb[:, j * page:(j + 1) * page], vt, _NT, preferred_element_type=F32)
        acc_scr[...] = acc

    @pl.when(c == n_chunks - 1)
    def _():
        full = jnp.where(own, acc_scr[...] / l_scr[:, 0:1], 0.0)
        out = full[0:n_new, :]
        for h in range(1, heads):
            out = out + full[h * n_new:(h + 1) * n_new, :]
        o_ref[...] = out.astype(BF16)


def _fox_paged(qn, kb, vb, c_new, kt_cache, vt_cache, lf_cache, layer, page_table, n_new):
    bsz, n_pages = page_table.shape
    _, n_pool, heads, hd, page = kt_cache.shape
    w = heads * hd
    g = _pick(n_pages, (16, 8, 4, 2, 1))
    n_chunks = n_pages // g
    rows = heads * n_new
    pt = page_table.reshape(-1)
    pos = jnp.arange(page)
    tri = (pos[:, None] > pos[None, :]).astype(BF16)
    ones = jnp.ones((page, page), BF16)

    def page_of(b, c, pt_ref, j):
        return pt_ref[b * n_pages + n_pages - 1 - (c * g + j)]

    tok = lambda b, c, pt_ref: (b, 0)
    const = lambda b, c, pt_ref: (0, 0)
    in_specs = [pl.BlockSpec((n_new, w), tok), pl.BlockSpec((n_new, w), tok), pl.BlockSpec((n_new, w), tok),
                pl.BlockSpec((n_new, LANES), tok),
                pl.BlockSpec((page, page), const), pl.BlockSpec((page, page), const)]
    in_specs += [pl.BlockSpec((None, None, heads, hd, page),
                              functools.partial(lambda b, c, pt_ref, j: (layer, page_of(b, c, pt_ref, j), 0, 0, 0), j=j))
                 for j in range(g)]
    in_specs += [pl.BlockSpec((None, None, heads, page),
                              functools.partial(lambda b, c, pt_ref, j: (layer, page_of(b, c, pt_ref, j), 0, 0), j=j))
                 for j in range(g)]
    in_specs += [pl.BlockSpec(memory_space=pl.ANY)]
    kern = functools.partial(_fox_paged_kernel, g=g, heads=heads, hd=hd, n_new=n_new, n_chunks=n_chunks,
                             n_pages=n_pages, layer=layer)
    grid_spec = pltpu.PrefetchScalarGridSpec(
        num_scalar_prefetch=1,
        grid=(bsz, n_chunks),
        in_specs=in_specs,
        out_specs=pl.BlockSpec((n_new, w), tok),
        scratch_shapes=[pltpu.VMEM((rows, LANES), F32), pltpu.VMEM((rows, LANES), F32),
                        pltpu.VMEM((rows, w), F32), pltpu.VMEM((heads, page), F32),
                        pltpu.VMEM((rows, w), BF16), pltpu.VMEM((rows, LANES), F32),
                        pltpu.VMEM((g, heads, hd, page), F32), pltpu.SemaphoreType.DMA((g,))],
    )
    return pl.pallas_call(
        kern,
        grid_spec=grid_spec,
        out_shape=jax.ShapeDtypeStruct((bsz * n_new, w), BF16),
        compiler_params=_params(2),
        name="fox_paged",
    )(pt, qn, kb, vb, c_new, tri, ones, *([kt_cache] * g), *([lf_cache] * g), vt_cache)


def _merge_kernel(x_ref, ca_ref, ra_ref, fo_ref, g0_ref, g1_ref, g2_ref, wpw_ref, bpw_ref, wro_ref, wfo_ref,
                  wout_ref, o_ref):
    y_conv = _dot(ca_ref[...], wpw_ref[...]) + bpw_ref[...]
    y_ret = _dot(ra_ref[...], wro_ref[...])
    y_fox = _dot(fo_ref[...], wfo_ref[...])
    merged = (_sigmoid(g0_ref[...].astype(F32)) * y_conv + _sigmoid(g1_ref[...].astype(F32)) * y_ret
              + _sigmoid(g2_ref[...].astype(F32)) * y_fox)
    o_ref[...] = x_ref[...] + _dot(merged.astype(BF16), wout_ref[...])


def _merge(x, z, ca, ra, fo, wpw, bpw, wro, wfo, wout, gate_col0):
    t, d = x.shape
    cw = ca.shape[1]
    tm = _pick(t, (512, 256))
    gc = gate_col0 // d
    row = lambda i: (i, 0)
    full = lambda i: (0, 0)
    return pl.pallas_call(
        _merge_kernel,
        grid=(t // tm,),
        in_specs=[
            pl.BlockSpec((tm, d), row),
            pl.BlockSpec((tm, cw), row), pl.BlockSpec((tm, cw), row), pl.BlockSpec((tm, cw), row),
            pl.BlockSpec((tm, d), lambda i: (i, gc)),
            pl.BlockSpec((tm, d), lambda i: (i, gc + 1)),
            pl.BlockSpec((tm, d), lambda i: (i, gc + 2)),
            pl.BlockSpec((cw, d), full), pl.BlockSpec((1, d), full),
            pl.BlockSpec((cw, d), full), pl.BlockSpec((cw, d), full), pl.BlockSpec((d, d), full),
        ],
        out_specs=pl.BlockSpec((tm, d), row),
        out_shape=jax.ShapeDtypeStruct((t, d), F32),
        compiler_params=_params(1),
        name="merge",
    )(x, ca, ra, fo, z, z, z, wpw, bpw, wro, wfo, wout)


def _mem_kv_kernel(m_ref, g_ref, wk_ref, wv_ref, gk_ref, k_ref, v_ref, *, heads, hd):
    mb = _rms(m_ref[...], g_ref[...]).astype(BF16)
    k = _dot(mb, wk_ref[...])
    k_ref[...] = jnp.concatenate(
        [_rms(k[:, h * hd:(h + 1) * hd], gk_ref[...]) for h in range(heads)], axis=1)
    v_ref[...] = _dot(mb, wv_ref[...])


def _mem_kv(mem, g, wk, wv, gk, heads, hd):
    t, d = mem.shape
    w = heads * hd
    tm = _pick(t, (512, 256, 128))
    full = lambda i: (0, 0)
    return pl.pallas_call(
        functools.partial(_mem_kv_kernel, heads=heads, hd=hd),
        grid=(t // tm,),
        in_specs=[pl.BlockSpec((tm, d), lambda i: (i, 0)), pl.BlockSpec((1, d), full),
                  pl.BlockSpec((d, w), full), pl.BlockSpec((d, w), full), pl.BlockSpec((1, hd), full)],
        out_specs=[pl.BlockSpec((tm, w), lambda i: (i, 0))] * 2,
        out_shape=[jax.ShapeDtypeStruct((t, w), F32)] * 2,
        compiler_params=_params(1),
        name="mem_kv",
    )(mem, g, wk, wv, gk)


def _cross_kernel(x_ref, gx_ref, wq_ref, gq_ref, mk_ref, mv_ref, wo_ref, o_ref, *, heads, hd, by_head):
    x = x_ref[...]
    tm = x.shape[0]
    q = _dot(_rms(x, gx_ref[...]).astype(BF16), wq_ref[...])
    qn = [_rms(q[:, h * hd:(h + 1) * hd], gq_ref[...]).astype(BF16) for h in range(heads)]
    if by_head:
        qs = jnp.concatenate(qn, axis=0)
        s = lax.dot_general(qs, mk_ref[...].astype(BF16), _NT, preferred_element_type=F32) * (hd ** -0.5)
        own = (lax.broadcasted_iota(jnp.int32, s.shape, 1) % heads
               == lax.broadcasted_iota(jnp.int32, s.shape, 0) // tm)
        s = jnp.where(own, s, NEG_INF)
        p = jnp.exp(s - jnp.max(s, axis=-1, keepdims=True))
        p = p / jnp.sum(p, axis=-1, keepdims=True)
        rows = _dot(p.astype(BF16), mv_ref[...].astype(BF16))
        outs = [rows[h * tm:(h + 1) * tm, :] for h in range(heads)]
    else:
        outs = []
        for h in range(heads):
            mk = mk_ref[:, h * hd:(h + 1) * hd].astype(BF16)
            mv = mv_ref[:, h * hd:(h + 1) * hd].astype(BF16)
            s = lax.dot_general(qn[h], mk, _NT, preferred_element_type=F32) * (hd ** -0.5)
            p = jnp.exp(s - jnp.max(s, axis=-1, keepdims=True))
            p = p / jnp.sum(p, axis=-1, keepdims=True)
            outs.append(_dot(p.astype(BF16), mv))
    o = jnp.concatenate(outs, axis=1).astype(BF16)
    o_ref[...] = x + _dot(o, wo_ref[...])


def _cross(x, gx, wq, gq, mem_k, mem_v, layer, wo, bsz, seq, heads, hd):
    t, d = x.shape
    w = heads * hd
    tm = _pick(seq, (512, 256, 128, 64, 32, 16, 8))
    n = seq // tm
    full = lambda b, i: (0, 0)
    by_head = mem_k.ndim == 5
    if by_head:
        depth, _, n_mem = mem_k.shape[:3]
        mem_k = mem_k.reshape(depth, bsz, n_mem * heads, hd)
        mem_v = mem_v.reshape(depth, bsz, n_mem * heads, hd)
        mem_spec = pl.BlockSpec((None, None, n_mem * heads, hd), lambda b, i: (layer, b, 0, 0))
    else:
        n_mem = mem_k.shape[1]
        mem_spec = pl.BlockSpec((None, n_mem, w), lambda b, i: (b, 0, 0))
    return pl.pallas_call(
        functools.partial(_cross_kernel, heads=heads, hd=hd, by_head=by_head),
        grid=(bsz, n),
        in_specs=[pl.BlockSpec((tm, d), lambda b, i: (b * n + i, 0)), pl.BlockSpec((1, d), full),
                  pl.BlockSpec((d, w), full), pl.BlockSpec((1, hd), full),
                  mem_spec, mem_spec,
                  pl.BlockSpec((w, d), full)],
        out_specs=pl.BlockSpec((tm, d), lambda b, i: (b * n + i, 0)),
        out_shape=jax.ShapeDtypeStruct((t, d), F32),
        compiler_params=_params(2),
        name="cross_attn",
    )(x, gx, wq, gq, mem_k, mem_v, wo)


def _ffn_kernel(x_ref, g_ref, w1_ref, w2_ref, o_ref, h_scr, acc_scr):
    k = pl.program_id(1)

    @pl.when(k == 0)
    def _():
        h_scr[...] = _rms(x_ref[...], g_ref[...]).astype(BF16)
        acc_scr[...] = jnp.zeros_like(acc_scr)

    a = jnp.maximum(_dot(h_scr[...], w1_ref[...]), 0.0)
    acc_scr[...] += _dot((a * a).astype(BF16), w2_ref[...])

    @pl.when(k == pl.num_programs(1) - 1)
    def _():
        o_ref[...] = x_ref[...] + acc_scr[...]


def _ffn(x, g, w1, w2):
    t, d = x.shape
    dff = w1.shape[1]
    tm = _pick(t, (1024, 512, 256))
    tf = 1024
    return pl.pallas_call(
        _ffn_kernel,
        grid=(t // tm, dff // tf),
        in_specs=[pl.BlockSpec((tm, d), lambda i, k: (i, 0)), pl.BlockSpec((1, d), lambda i, k: (0, 0)),
                  pl.BlockSpec((d, tf), lambda i, k: (0, k)), pl.BlockSpec((tf, d), lambda i, k: (k, 0))],
        out_specs=pl.BlockSpec((tm, d), lambda i, k: (i, 0)),
        out_shape=jax.ShapeDtypeStruct((t, d), F32),
        scratch_shapes=[pltpu.VMEM((tm, d), BF16), pltpu.VMEM((tm, d), F32)],
        compiler_params=_params(2),
        name="ffn",
    )(x, g, w1, w2)


def _layer(x, bsz, seq, conv_hist, ret_state0, ret_tables, mem_k, mem_v, layer, lw, dims, fox):
    d_conv, ret_heads, ret_dk, ret_dv, fox_heads, fox_hd, x_heads, x_hd = dims
    prompt = fox[0] == "prompt"
    z_dtype = BF16 if seq % BF16_ROWS == 0 else F32
    z, lf = _in_proj(x, lw["g_mix"], lw["w_main"], lw["b_main"], lw["w_f"], lw["b_f"], z_dtype)
    c_act, new_buf = _conv_branch(z, conv_hist, lw["conv_w"], lw["conv_b"], lw["conv_ln_g"], lw["conv_ln_b"],
                                  bsz, seq)
    ret_col0 = 2 * d_conv
    r_act, new_state = _ret_branch(z, ret_tables, lw["ret_gn"], ret_state0, bsz, seq, ret_col0)
    fox_col0 = ret_col0 + 2 * ret_heads * ret_dk + 2 * ret_heads * ret_dv
    if prompt:
        _, kbuf, vbuf = fox
        qw, kw, vw, kbuf, vbuf = _fox_prep_prompt(z, lf, lw["fox_g_q"], lw["fox_g_k"], kbuf, vbuf, layer,
                                                  bsz, seq, fox_col0, fox_heads, fox_hd)
        fo = _fox_flash(qw, kw, vw, bsz, seq, fox_heads, fox_hd, FLASH_HEADS)
        fox_out = (kbuf, vbuf)
    else:
        _, kt_cache, vt_cache, lf_cache, page_table = fox
        qn, kf, kb, vf, vb, c = _fox_prep(z, lf, lw["fox_g_q"], lw["fox_g_k"], bsz, seq, fox_col0,
                                          fox_heads, fox_hd)
        fo = _fox_paged(qn, kb, vb, c, kt_cache, vt_cache, lf_cache, layer, page_table, seq)
        fox_out = (kf, vf)
    gate_col0 = fox_col0 + 3 * fox_heads * fox_hd
    x = _merge(x, z, c_act, r_act, fo, lw["conv_w_pw"], lw["conv_b_pw"], lw["ret_w_o"], lw["fox_w_o"],
               lw["w_out"], gate_col0)
    x = _cross(x, lw["g_x"], lw["x_w_q"], lw["x_g_q"], mem_k, mem_v, layer, lw["x_w_o"], bsz, seq, x_heads, x_hd)
    x = _ffn(x, lw["g_ff"], lw["w_ff1"], lw["w_ff2"])
    return x, new_buf, new_state, fox_out, lf[:, :fox_heads]


def kernel(x_prompt, x_sample, cache_conv, state_ret, cache_fox_k, cache_fox_v, cache_fox_logf, cache_mem_k, cache_mem_v, page_table, mem_prompt, g_mix, w_in, b_in, conv_w, conv_b, conv_ln_g, conv_ln_b, conv_w_pw, conv_b_pw, ret_gn, ret_w_o, fox_g_q, fox_g_k, fox_w_o, w_out, g_x, g_mem, x_w_q, x_w_k, x_w_v, x_g_q, x_g_k, x_w_o, g_ff, w_ff1, w_ff2):
    bp, sp, d = x_prompt.shape
    bs, ss, _ = x_sample.shape
    depth = w_in.shape[0]
    width, d_conv = conv_w.shape[1], conv_w.shape[2]
    ret_heads, ret_dk, ret_dv = state_ret.shape[2], state_ret.shape[3], state_ret.shape[4]
    n_pool, page, fox_heads, fox_hd = cache_fox_k.shape[1:]
    n_mem, x_heads, x_hd = cache_mem_k.shape[2:]
    n_pages = page_table.shape[1]
    past = n_pages * page
    dims = (d_conv, ret_heads, ret_dk, ret_dv, fox_heads, fox_hd, x_heads, x_hd)
    fw = fox_heads * fox_hd
    forget_col = 2 * d_conv + 2 * ret_heads * ret_dk + 2 * ret_heads * ret_dv + 3 * fw

    tables_p = _ret_tables(sp, 0, ret_heads, ret_dk, ret_dv)
    tables_s = _ret_tables(ss, past, ret_heads, ret_dk, ret_dv)
    conv0 = jnp.zeros((bp, width - 1, d_conv), F32)
    ret0 = jnp.zeros((bp, ret_heads, ret_dk, ret_dv), F32)
    kt_cache = jnp.transpose(cache_fox_k, (0, 1, 3, 4, 2))
    vt_cache = jnp.transpose(cache_fox_v, (0, 1, 3, 4, 2))
    lf_cache = jnp.transpose(cache_fox_logf, (0, 1, 3, 2))
    kbuf = jnp.zeros((depth, bp, fw, sp), F32)
    vbuf = jnp.zeros((depth, bp, fw, sp), F32)

    xp = x_prompt.reshape(bp * sp, d)
    xs = x_sample.reshape(bs * ss, d)
    mem2d = mem_prompt.reshape(bp * n_mem, d)
    outs_p = [[] for _ in range(5)]
    outs_s = [[] for _ in range(5)]
    row = lambda a: a.reshape(1, -1)
    for l in range(depth):
        w_f = jnp.pad(w_in[l][:, forget_col:forget_col + fox_heads], ((0, 0), (0, LANES - fox_heads)))
        b_f = jnp.pad(b_in[l][forget_col:forget_col + fox_heads], (0, LANES - fox_heads))
        lw = {
            "g_mix": row(g_mix[l]),
            "w_main": jnp.concatenate([w_in[l][:, :forget_col], w_in[l][:, forget_col + fox_heads:]],
                                      axis=1).astype(BF16),
            "b_main": row(jnp.concatenate([b_in[l][:forget_col], b_in[l][forget_col + fox_heads:]])),
            "w_f": w_f.astype(BF16), "b_f": row(b_f),
            "conv_w": conv_w[l], "conv_b": row(conv_b[l]),
            "conv_ln_g": row(conv_ln_g[l]), "conv_ln_b": row(conv_ln_b[l]),
            "conv_w_pw": conv_w_pw[l].astype(BF16), "conv_b_pw": row(conv_b_pw[l]),
            "ret_gn": row(ret_gn[l]), "ret_w_o": ret_w_o[l].astype(BF16),
            "fox_g_q": row(jnp.tile(fox_g_q[l], fox_heads)), "fox_g_k": row(jnp.tile(fox_g_k[l], fox_heads)),
            "fox_w_o": fox_w_o[l].astype(BF16), "w_out": w_out[l].astype(BF16),
            "g_x": row(g_x[l]), "x_w_q": x_w_q[l].astype(BF16), "x_g_q": row(x_g_q[l]),
            "x_w_o": x_w_o[l].astype(BF16), "g_ff": row(g_ff[l]),
            "w_ff1": w_ff1[l].astype(BF16), "w_ff2": w_ff2[l].astype(BF16),
        }
        mk, mv = _mem_kv(mem2d, row(g_mem[l]), x_w_k[l].astype(BF16), x_w_v[l].astype(BF16), row(x_g_k[l]),
                         x_heads, x_hd)
        mk3 = mk.reshape(bp, n_mem, x_heads * x_hd)
        mv3 = mv.reshape(bp, n_mem, x_heads * x_hd)
        xp, c_b, r_s, (kbuf, vbuf), l_f = _layer(xp, bp, sp, conv0, ret0, tables_p, mk3, mv3, l, lw, dims,
                                                 ("prompt", kbuf, vbuf))
        for lst, val in zip(outs_p, (c_b, r_s, l_f.reshape(bp, sp, fox_heads),
                                     mk.reshape(bp, n_mem, x_heads, x_hd), mv.reshape(bp, n_mem, x_heads, x_hd))):
            lst.append(val)
        xs, c_b, r_s, (f_k, f_v), l_f = _layer(xs, bs, ss, cache_conv[l], state_ret[l], tables_s,
                                               cache_mem_k, cache_mem_v, l, lw, dims,
                                               ("sample", kt_cache, vt_cache, lf_cache, page_table))
        for lst, val in zip(outs_s, (c_b, r_s, f_k.reshape(bs, ss, fox_heads, fox_hd),
                                     f_v.reshape(bs, ss, fox_heads, fox_hd), l_f.reshape(bs, ss, fox_heads))):
            lst.append(val)
    conv_p, ret_p, lf_p, mk_p, mv_p = [jnp.stack(o) for o in outs_p]
    fk_p = jnp.transpose(kbuf.reshape(depth, bp, fox_heads, fox_hd, sp), (0, 1, 4, 2, 3))
    fv_p = jnp.transpose(vbuf.reshape(depth, bp, fox_heads, fox_hd, sp), (0, 1, 4, 2, 3))
    return (xp.reshape(bp, sp, d), xs.reshape(bs, ss, d), conv_p, ret_p, fk_p, fv_p, lf_p, mk_p, mv_p,
            *[jnp.stack(o) for o in outs_s])
```

```python
import functools
import math

import jax
import jax.numpy as jnp
from jax import lax
from jax.experimental import pallas as pl
from jax.experimental.pallas import tpu as pltpu

F32 = jnp.float32
BF16 = jnp.bfloat16
EPS = 1e-6
ROPE_BASE = 10000.0
RET_CHUNK = 128
LOG2E = 1.4426950408889634
LANES = 128
SUBLANES = 8
BF16_ROWS = 16
VMEM_LIMIT = 56 * 1024 * 1024
NEG_INF = float("-inf")
FLASH_HEADS = 4

_NT = (((1,), (1,)), ((), ()))


def _params(n_grid):
    return pltpu.CompilerParams(dimension_semantics=("arbitrary",) * n_grid,
                                vmem_limit_bytes=VMEM_LIMIT)


def _split2(x):
    hi = x.astype(BF16)
    lo = (x - hi.astype(F32)).astype(BF16)
    return hi, lo


def _split3(x):
    hi = x.astype(BF16)
    r = x - hi.astype(F32)
    mid = r.astype(BF16)
    lo = (r - mid.astype(F32)).astype(BF16)
    return hi, mid, lo


def _dot(a, b):
    return jnp.dot(a, b, preferred_element_type=F32)


def _rms(x, g):
    return x * lax.rsqrt(jnp.mean(x * x, axis=-1, keepdims=True) + EPS) * g


def _log_sigmoid(x):
    return jnp.minimum(x, 0.0) - jnp.log1p(jnp.exp(-jnp.abs(x)))


def _sigmoid(x):
    return 1.0 / (1.0 + jnp.exp(-x))


def _pick(n, cands):
    for c in cands:
        if n % c == 0:
            return c
    return n


def _head_norm(x, g, gm, hd, two_piece=True):
    if two_piece:
        hi, lo = _split2(x * x)
        ms = (_dot(hi, gm) + _dot(lo, gm)) * (1.0 / hd)
    else:
        ms = _dot((x * x).astype(BF16), gm) * (1.0 / hd)
    return x * lax.rsqrt(ms + EPS) * g


def _cumsum_rows(x, tri):
    h1, h2, h3 = _split3(x)
    return _dot(tri, h1) + _dot(tri, h2) + _dot(tri, h3)


def _in_proj_kernel(x_ref, g_ref, w_ref, b_ref, wf_ref, bf_ref, z_ref, lf_ref, h_scr):
    @pl.when(pl.program_id(1) == 0)
    def _():
        hb = _rms(x_ref[...], g_ref[...]).astype(BF16)
        h_scr[...] = hb
        lf_ref[...] = _log_sigmoid(_dot(hb, wf_ref[...]) + bf_ref[...])

    z_ref[...] = (_dot(h_scr[...], w_ref[...]) + b_ref[...]).astype(z_ref.dtype)


def _in_proj(x, g, w, b, wf, bf, z_dtype):
    t, d = x.shape
    n = w.shape[1]
    tm = _pick(t, (1024, 512, 256))
    tn = _pick(n, (1792, 1024))
    return pl.pallas_call(
        _in_proj_kernel,
        grid=(t // tm, n // tn),
        in_specs=[
            pl.BlockSpec((tm, d), lambda i, j: (i, 0)),
            pl.BlockSpec((1, d), lambda i, j: (0, 0)),
            pl.BlockSpec((d, tn), lambda i, j: (0, j)),
            pl.BlockSpec((1, tn), lambda i, j: (0, j)),
            pl.BlockSpec((d, LANES), lambda i, j: (0, 0)),
            pl.BlockSpec((1, LANES), lambda i, j: (0, 0)),
        ],
        out_specs=[
            pl.BlockSpec((tm, tn), lambda i, j: (i, j)),
            pl.BlockSpec((tm, LANES), lambda i, j: (i, 0)),
        ],
        out_shape=[jax.ShapeDtypeStruct((t, n), z_dtype), jax.ShapeDtypeStruct((t, LANES), F32)],
        scratch_shapes=[pltpu.VMEM((tm, d), BF16)],
        compiler_params=_params(2),
        name="in_proj",
    )(x, g, w, b, wf, bf)


def _conv_kernel(a_ref, b_ref, hist_ref, w_ref, cb_ref, lg_ref, lb_ref, shift_ref, act_ref, buf_ref, ext_scr,
                 *, tt, n_tiles, width, sub, pad):
    i = pl.program_id(1)
    hist = width - 1
    off = pad - hist
    kwin = sub + pad

    @pl.when(i == 0)
    def _():
        ext_scr[0:pad, :] = jnp.zeros((pad, ext_scr.shape[1]), F32)
        ext_scr[off:pad, :] = hist_ref[0]

    ext_scr[pad:pad + tt, :] = a_ref[...].astype(F32) * _sigmoid(b_ref[...].astype(F32))
    for r0 in range(0, tt, sub):
        win = ext_scr[r0:r0 + kwin, :]
        hi, lo = _split2(win)
        shifted = _dot(shift_ref[...], hi) + _dot(shift_ref[...], lo)
        parts = []
        for cs in (slice(c0, c0 + LANES) for c0 in range(0, win.shape[1], LANES)):
            acc = None
            for j in range(width):
                r, base = (off + j) % SUBLANES, (off + j) // SUBLANES * SUBLANES
                if r == 0:
                    rows = win[base:base + sub, cs]
                else:
                    rows = shifted[(r - 1) * kwin + base:(r - 1) * kwin + base + sub, cs]
                term = rows * w_ref[j:j + 1, cs]
                acc = term if acc is None else acc + term
            parts.append(acc)
        y = jnp.concatenate(parts, axis=1) + cb_ref[...]
        yc = y - jnp.mean(y, axis=-1, keepdims=True)
        yn = yc * lax.rsqrt(jnp.mean(yc * yc, axis=-1, keepdims=True) + EPS) * lg_ref[...] + lb_ref[...]
        act_ref[r0:r0 + sub, :] = (yn * _sigmoid(yn)).astype(BF16)

    @pl.when(i == n_tiles - 1)
    def _():
        buf_ref[0] = ext_scr[tt + off:tt + pad, :]

    if n_tiles > 1:
        ext_scr[off:pad, :] = ext_scr[tt + off:tt + pad, :]


def _conv_branch(z, hist, w, cb, lg, lb, bsz, seq):
    c = w.shape[1]
    width = w.shape[0]
    tt = _pick(seq, (256, 128, 64, 32, 16, 8))
    n_tiles = seq // tt
    sub = min(tt, 128)
    pad = 32
    assert width - 1 <= pad and (n_tiles == 1 or tt >= width - 1)
    kwin = sub + pad
    src = jnp.arange(kwin)
    shift = jnp.concatenate(
        [((src[:, None] + r == src[None, :]) & (src[:, None] + r < kwin)) for r in range(1, SUBLANES)],
        axis=0).astype(BF16)
    kern = functools.partial(_conv_kernel, tt=tt, n_tiles=n_tiles, width=width, sub=sub, pad=pad)
    return pl.pallas_call(
        kern,
        grid=(bsz, n_tiles),
        in_specs=[
            pl.BlockSpec((tt, c), lambda b, i: (b * n_tiles + i, 0)),
            pl.BlockSpec((tt, c), lambda b, i: (b * n_tiles + i, 1)),
            pl.BlockSpec((1, width - 1, c), lambda b, i: (b, 0, 0)),
            pl.BlockSpec((width, c), lambda b, i: (0, 0)),
            pl.BlockSpec((1, c), lambda b, i: (0, 0)),
            pl.BlockSpec((1, c), lambda b, i: (0, 0)),
            pl.BlockSpec((1, c), lambda b, i: (0, 0)),
            pl.BlockSpec(((SUBLANES - 1) * kwin, kwin), lambda b, i: (0, 0)),
        ],
        out_specs=[
            pl.BlockSpec((tt, c), lambda b, i: (b * n_tiles + i, 0)),
            pl.BlockSpec((1, width - 1, c), lambda b, i: (b, 0, 0)),
        ],
        out_shape=[jax.ShapeDtypeStruct((bsz * seq, c), BF16),
                   jax.ShapeDtypeStruct((bsz, width - 1, c), F32)],
        scratch_shapes=[pltpu.VMEM((pad + tt, c), F32)],
        compiler_params=_params(2),
        name="conv_branch",
    )(z, z, hist, w, cb, lg, lb, shift)


def _ret_kernel(q_ref, k_ref, v_ref, rg_ref, cos_ref, sin_ref, dm_ref, qd_ref, kd_ref, cd_ref, gn_ref,
                s0_ref, act_ref, s_ref, *, heads, dk, dv, group):
    @pl.when(pl.program_id(1) == 0)
    def _():
        s_ref[...] = s0_ref[...]

    cos = cos_ref[...]
    sin = sin_ref[...]
    half = dk // 2

    def rot(x):
        lane = lax.broadcasted_iota(jnp.int32, x.shape, 1)
        upper = (lane % dk) >= half
        swapped = jnp.where(upper, pltpu.roll(x, half, 1), pltpu.roll(x, x.shape[1] - half, 1))
        return x * cos + swapped * sin

    for bb in range(group):
        q = rot(q_ref[bb].astype(F32)) * (dk ** -0.5)
        k = rot(k_ref[bb].astype(F32))
        qb = q.astype(BF16)
        kb = k.astype(BF16)
        kdt = (k * kd_ref[...]).T.astype(BF16)
        vb = v_ref[bb].astype(BF16)
        outs = []
        for h in range(heads):
            qh = qb[:, h * dk:(h + 1) * dk]
            kh = kb[:, h * dk:(h + 1) * dk]
            vh = vb[:, h * dv:(h + 1) * dv]
            inner = lax.dot_general(qh, kh, _NT, preferred_element_type=F32) * dm_ref[h]
            s_old = s_ref[bb, h]
            o = _dot(inner.astype(BF16), vh) + _dot(qh, s_old.astype(BF16)) * qd_ref[:, h * dv:(h + 1) * dv]
            s_ref[bb, h] = s_old * cd_ref[h] + _dot(kdt[h * dk:(h + 1) * dk, :], vh)
            outs.append(_rms(o, gn_ref[:, h * dv:(h + 1) * dv]))
        o_all = jnp.concatenate(outs, axis=1)
        rg = rg_ref[bb].astype(F32)
        act_ref[bb] = (rg * _sigmoid(rg) * o_all).astype(BF16)


def _ret_tables(seq, pos0, heads, dk, dv):
    c = math.gcd(seq, RET_CHUNK)
    half = dk // 2
    inv_freq = jnp.exp(-math.log(ROPE_BASE) * jnp.arange(half, dtype=F32) / half)
    pos = (pos0 + jnp.arange(seq)).astype(F32)
    ang = pos[:, None] * inv_freq[None, :]
    cos = jnp.tile(jnp.concatenate([jnp.cos(ang), jnp.cos(ang)], axis=-1), (1, heads))
    sin = jnp.tile(jnp.concatenate([-jnp.sin(ang), jnp.sin(ang)], axis=-1), (1, heads))
    log_g = jnp.log1p(-jnp.exp2(-5.0 - jnp.arange(heads, dtype=F32)))
    idx = jnp.arange(c, dtype=F32)
    diff = idx[:, None] - idx[None, :]
    dmask = jnp.where(diff >= 0, jnp.exp(log_g[:, None, None] * jnp.maximum(diff, 0.0)), 0.0)
    q_dec = jnp.exp(log_g[:, None] * (idx[None, :] + 1.0))
    k_dec = jnp.exp(log_g[:, None] * (c - 1.0 - idx[None, :]))
    chunk_dec = jnp.exp(log_g * c)
    qd = jnp.repeat(q_dec.T, dv, axis=1)
    kd = jnp.repeat(k_dec.T, dk, axis=1)
    cd = jnp.broadcast_to(chunk_dec[:, None, None], (heads, 1, dv))
    return c, cos, sin, dmask, qd, kd, cd


def _ret_branch(z, tables, gn, state0, bsz, seq, col0):
    heads, dk, dv = state0.shape[1], state0.shape[2], state0.shape[3]
    c, cos, sin, dmask, qd, kd, cd = tables
    n = seq // c
    hk, hv = heads * dk, heads * dv
    qcol, kcol, vcol, gcol = col0 // hk, col0 // hk + 1, (col0 + 2 * hk) // hv, (col0 + 2 * hk) // hv + 1
    group = _pick(bsz, (4, 2, 1))
    z3 = z.reshape(bsz, seq, z.shape[1])
    kern = functools.partial(_ret_kernel, heads=heads, dk=dk, dv=dv, group=group)
    act, state = pl.pallas_call(
        kern,
        grid=(bsz // group, n),
        in_specs=[
            pl.BlockSpec((group, c, hk), lambda g, i: (g, i, qcol)),
            pl.BlockSpec((group, c, hk), lambda g, i: (g, i, kcol)),
            pl.BlockSpec((group, c, hv), lambda g, i: (g, i, vcol)),
            pl.BlockSpec((group, c, hv), lambda g, i: (g, i, gcol)),
            pl.BlockSpec((c, hk), lambda g, i: (i, 0)),
            pl.BlockSpec((c, hk), lambda g, i: (i, 0)),
            pl.BlockSpec((heads, c, c), lambda g, i: (0, 0, 0)),
            pl.BlockSpec((c, hv), lambda g, i: (0, 0)),
            pl.BlockSpec((c, hk), lambda g, i: (0, 0)),
            pl.BlockSpec((heads, 1, dv), lambda g, i: (0, 0, 0)),
            pl.BlockSpec((1, hv), lambda g, i: (0, 0)),
            pl.BlockSpec((group, heads, dk, dv), lambda g, i: (g, 0, 0, 0)),
        ],
        out_specs=[
            pl.BlockSpec((group, c, hv), lambda g, i: (g, i, 0)),
            pl.BlockSpec((group, heads, dk, dv), lambda g, i: (g, 0, 0, 0)),
        ],
        out_shape=[jax.ShapeDtypeStruct((bsz, seq, hv), BF16),
                   jax.ShapeDtypeStruct(state0.shape, F32)],
        compiler_params=_params(2),
        name="retention",
    )(z3, z3, z3, z3, cos, sin, dmask, qd, kd, cd, gn, state0)
    return act.reshape(bsz * seq, hv), state


def _fox_prep_kernel(fq_ref, fk_ref, fv_ref, lf_ref, gq_ref, gk_ref, gm_ref, tri_ref,
                     qn_ref, kf_ref, kb_ref, vf_ref, vb_ref, c_ref, carry, *, hd):
    @pl.when(pl.program_id(1) == 0)
    def _():
        carry[...] = jnp.zeros_like(carry)

    gm = gm_ref[...]
    qn_ref[...] = (_head_norm(fq_ref[...].astype(F32), gq_ref[...], gm, hd) * (hd ** -0.5)).astype(BF16)
    kn = _head_norm(fk_ref[...].astype(F32), gk_ref[...], gm, hd)
    kf_ref[...] = kn
    kb_ref[...] = kn.astype(BF16)
    v = fv_ref[...].astype(F32)
    vf_ref[...] = v
    vb_ref[...] = v.astype(BF16)
    cs = _cumsum_rows(lf_ref[...], tri_ref[...]) + carry[0:1, :]
    c_ref[...] = cs
    tm = cs.shape[0]
    carry[...] = jnp.broadcast_to(cs[tm - 1:tm, :], carry.shape)


def _fox_consts(heads, hd, tm):
    gm = jnp.kron(jnp.eye(heads, dtype=F32), jnp.ones((hd, hd), F32)).astype(BF16)
    tri = jnp.tril(jnp.ones((tm, tm), F32)).astype(BF16)
    return gm, tri


def _fox_prep(z, lf, gq, gk, bsz, seq, col0, heads, hd):
    t = bsz * seq
    w = heads * hd
    tm = _pick(seq, (512, 256, 128, 64, 32, 16, 8))
    n = seq // tm
    qcol = col0 // w
    gm, tri = _fox_consts(heads, hd, tm)
    row = lambda b, i: (b * n + i, 0)
    full = lambda b, i: (0, 0)
    return pl.pallas_call(
        functools.partial(_fox_prep_kernel, hd=hd),
        grid=(bsz, n),
        in_specs=[
            pl.BlockSpec((tm, w), lambda b, i: (b * n + i, qcol)),
            pl.BlockSpec((tm, w), lambda b, i: (b * n + i, qcol + 1)),
            pl.BlockSpec((tm, w), lambda b, i: (b * n + i, qcol + 2)),
            pl.BlockSpec((tm, LANES), row),
            pl.BlockSpec((1, w), full), pl.BlockSpec((1, w), full),
            pl.BlockSpec((w, w), full), pl.BlockSpec((tm, tm), full),
        ],
        out_specs=[pl.BlockSpec((tm, w), row)] * 5 + [pl.BlockSpec((tm, LANES), row)],
        out_shape=[jax.ShapeDtypeStruct((t, w), BF16), jax.ShapeDtypeStruct((t, w), F32),
                   jax.ShapeDtypeStruct((t, w), BF16), jax.ShapeDtypeStruct((t, w), F32),
                   jax.ShapeDtypeStruct((t, w), BF16), jax.ShapeDtypeStruct((t, LANES), F32)],
        scratch_shapes=[pltpu.VMEM((SUBLANES, LANES), F32)],
        compiler_params=_params(2),
        name="fox_prep",
    )(z, z, z, lf, gq, gk, gm, tri)


def _fox_prep_prompt_kernel(fq_ref, fk_ref, fv_ref, lf_ref, gq_ref, gk_ref, gm_ref, tri_ref,
                            cq_ref, ck_ref, rows_ref, kbuf_ref, vbuf_ref,
                            q_ref, k_ref, v_ref, kt_ref, vt_ref, carry, *, hd, heads):
    del kbuf_ref, vbuf_ref

    @pl.when(pl.program_id(1) == 0)
    def _():
        carry[...] = jnp.zeros_like(carry)

    gm = gm_ref[...]

    def spread(x):
        gap = jnp.zeros((x.shape[0], LANES - hd), x.dtype)
        return jnp.concatenate([piece for h in range(x.shape[1] // hd)
                                for piece in (x[:, h * hd:(h + 1) * hd], gap)], axis=1)

    qn = _head_norm(fq_ref[...].astype(F32), gq_ref[...], gm, hd, two_piece=False) * (LOG2E * hd ** -0.5)
    kn = _head_norm(fk_ref[...].astype(F32), gk_ref[...], gm, hd)
    v = fv_ref[...].astype(F32)
    cs = _cumsum_rows(lf_ref[...], tri_ref[...]) + carry[0:1, :]
    tm = cs.shape[0]
    carry[...] = jnp.broadcast_to(cs[tm - 1:tm, :], carry.shape)
    lane = lax.broadcasted_iota(jnp.int32, cs.shape, 1)
    c1, c2, c3 = _split3(jnp.where(lane < heads, cs * LOG2E, 0.0))
    pieces = (c1.astype(F32) + pltpu.roll(c2.astype(F32), heads, 1)
              + pltpu.roll(c3.astype(F32), 2 * heads, 1)).astype(BF16)
    q_ref[...] = (spread(qn) + _dot(pieces, cq_ref[...]) + rows_ref[0:1, :]).astype(BF16)
    k_ref[...] = (spread(kn) + _dot(pieces, ck_ref[...]) + rows_ref[1:2, :]).astype(BF16)
    v_ref[...] = (spread(v) + rows_ref[2:3, :]).astype(BF16)
    kt_ref[...] = kn.T
    vt_ref[...] = v.T


def _fox_prep_prompt(z, lf, gq, gk, kbuf, vbuf, layer, bsz, seq, col0, heads, hd):
    t = bsz * seq
    w = heads * hd
    wide = heads * LANES
    tm = _pick(seq, (512, 256, 128))
    n = seq // tm
    qcol = col0 // w
    gm, tri = _fox_consts(heads, hd, tm)
    src = jnp.arange(LANES)[:, None]
    dst = jnp.arange(wide)[None, :]
    assert 3 * heads <= LANES

    def lane_sel(first):
        return (dst == (src % heads) * LANES + first + src // heads) & (src < 3 * heads)

    cq = lane_sel(hd).astype(BF16)
    ck = -lane_sel(hd + 3).astype(BF16)
    lane_in_head = jnp.arange(wide) % LANES
    rows = jnp.stack([((lane_in_head >= hd + 3) & (lane_in_head < hd + 6)).astype(F32),
                      ((lane_in_head >= hd) & (lane_in_head < hd + 3)).astype(F32),
                      (lane_in_head == hd).astype(F32)] + [jnp.zeros((wide,), F32)] * (SUBLANES - 3))
    row = lambda b, i: (b * n + i, 0)
    full = lambda b, i: (0, 0)
    tspec = pl.BlockSpec((None, None, w, tm), lambda b, i: (layer, b, 0, i))
    return pl.pallas_call(
        functools.partial(_fox_prep_prompt_kernel, hd=hd, heads=heads),
        grid=(bsz, n),
        in_specs=[
            pl.BlockSpec((tm, w), lambda b, i: (b * n + i, qcol)),
            pl.BlockSpec((tm, w), lambda b, i: (b * n + i, qcol + 1)),
            pl.BlockSpec((tm, w), lambda b, i: (b * n + i, qcol + 2)),
            pl.BlockSpec((tm, LANES), row),
            pl.BlockSpec((1, w), full), pl.BlockSpec((1, w), full),
            pl.BlockSpec((w, w), full), pl.BlockSpec((tm, tm), full),
            pl.BlockSpec((LANES, wide), full), pl.BlockSpec((LANES, wide), full),
            pl.BlockSpec((SUBLANES, wide), full),
            pl.BlockSpec(memory_space=pl.ANY), pl.BlockSpec(memory_space=pl.ANY),
        ],
        out_specs=[pl.BlockSpec((tm, wide), row)] * 3 + [tspec, tspec],
        out_shape=[jax.ShapeDtypeStruct((t, wide), BF16)] * 3
        + [jax.ShapeDtypeStruct(kbuf.shape, F32), jax.ShapeDtypeStruct(vbuf.shape, F32)],
        input_output_aliases={11: 3, 12: 4},
        scratch_shapes=[pltpu.VMEM((SUBLANES, LANES), F32)],
        compiler_params=_params(2),
        name="fox_prep_prompt",
    )(z, z, z, lf, gq, gk, gm, tri, cq, ck, rows, kbuf, vbuf)


def _fox_flash_kernel(q_ref, k_ref, v_ref, o_ref, *, tq, tk, hd, nh):
    i = pl.program_id(2)
    r = tq // tk
    lane = lax.broadcasted_iota(jnp.int32, (tq, LANES), 1)

    def update(m_old, acc, s, v_blk):
        m_new = jnp.maximum(m_old, jnp.max(s, axis=-1, keepdims=True))
        p = jnp.exp2(s - m_new)
        return m_new, acc * jnp.exp2(m_old - m_new) + _dot(p.astype(BF16), v_blk)

    def full_block(j, carry):
        start = pl.multiple_of(j * tk, tk)
        out = []
        for e in range(nh):
            cols = slice(e * LANES, (e + 1) * LANES)
            s = lax.dot_general(q_ref[:, cols], k_ref[pl.ds(start, tk), cols], _NT, preferred_element_type=F32)
            out.append(update(*carry[e], s, v_ref[pl.ds(start, tk), cols]))
        return tuple(out)

    init = ((jnp.full((tq, 1), NEG_INF, F32), jnp.zeros((tq, LANES), F32)),) * nh
    carry = list(lax.fori_loop(0, i * r, full_block, init))
    visible = (lax.broadcasted_iota(jnp.int32, (tk, tk), 1) <= lax.broadcasted_iota(jnp.int32, (tk, tk), 0))
    for d in range(r):
        start = pl.multiple_of((i * r + d) * tk, tk)
        lo = d * tk
        for e in range(nh):
            cols = slice(e * LANES, (e + 1) * LANES)
            m_old, acc = carry[e]
            s = lax.dot_general(q_ref[lo:, cols], k_ref[pl.ds(start, tk), cols], _NT, preferred_element_type=F32)
            top = jnp.where(visible, s[:tk], NEG_INF)
            s = top if tq - lo == tk else jnp.concatenate([top, s[tk:]], axis=0)
            m_new, acc_new = update(m_old[lo:], acc[lo:], s, v_ref[pl.ds(start, tk), cols])
            if lo:
                m_new = jnp.concatenate([m_old[:lo], m_new], axis=0)
                acc_new = jnp.concatenate([acc[:lo], acc_new], axis=0)
            carry[e] = (m_new, acc_new)
    outs = []
    for e in range(nh):
        acc = carry[e][1]
        denom = jnp.sum(jnp.where(lane == hd, acc, 0.0), axis=-1, keepdims=True)
        outs.append(acc / denom)
    for e in range(0, nh, 2):
        o_ref[:, e // 2 * LANES:(e // 2 + 1) * LANES] = jnp.where(
            lane < hd, outs[e], pltpu.roll(outs[e + 1], hd, 1)).astype(BF16)


def _fox_flash(qw, kw, vw, bsz, seq, heads, hd, nh):
    t = bsz * seq
    tk = _pick(seq, (512, 256, 128))
    tq = _pick(seq, (2 * tk, tk))
    nq = seq // tq
    kern = functools.partial(_fox_flash_kernel, tq=tq, tk=tk, hd=hd, nh=nh)
    return pl.pallas_call(
        kern,
        grid=(bsz, heads // nh, nq),
        in_specs=[
            pl.BlockSpec((tq, nh * LANES), lambda b, h, i: (b * nq + i, h)),
            pl.BlockSpec((seq, nh * LANES), lambda b, h, i: (b, h)),
            pl.BlockSpec((seq, nh * LANES), lambda b, h, i: (b, h)),
        ],
        out_specs=pl.BlockSpec((tq, nh * hd), lambda b, h, i: (b * nq + i, h)),
        out_shape=jax.ShapeDtypeStruct((t, heads * hd), BF16),
        compiler_params=_params(3),
        name="fox_flash",
    )(qw, kw, vw)


def _fox_paged_kernel(pt_ref, q_ref, kn_ref, vn_ref, cn_ref, tri_ref, ones_ref, *refs,
                      g, heads, hd, n_new, n_chunks, n_pages, layer):
    k_refs = refs[0:g]
    lf_refs = refs[g:2 * g]
    v_hbm = refs[2 * g]
    o_ref, m_scr, l_scr, acc_scr, carry_scr, qbd_scr, rowc_scr, v_buf, v_sem = refs[2 * g + 1:]
    b = pl.program_id(0)
    c = pl.program_id(1)
    rows = heads * n_new
    w = heads * hd
    page = k_refs[0].shape[-1]
    own = (lax.broadcasted_iota(jnp.int32, (rows, w), 0) // n_new
           == lax.broadcasted_iota(jnp.int32, (rows, w), 1) // hd)

    @pl.when(c == 0)
    def _():
        q = q_ref[...].astype(F32)
        qbd = jnp.where(own, jnp.concatenate([q] * heads, axis=0), 0.0).astype(BF16)
        qbd_scr[...] = qbd
        cn = cn_ref[...]
        sel = (lax.broadcasted_iota(jnp.int32, (rows, LANES), 1)
               == lax.broadcasted_iota(jnp.int32, (rows, LANES), 0) // n_new)
        rowc = jnp.sum(jnp.where(sel, jnp.concatenate([cn] * heads, axis=0), 0.0), axis=-1, keepdims=True)
        rowc_scr[...] = jnp.broadcast_to(rowc, rowc_scr.shape)
        selb = jnp.where(sel, 1.0, 0.0).astype(BF16)
        c1, c2, c3 = _split3(cn)
        colc = (lax.dot_general(selb, c1, _NT, preferred_element_type=F32)
                + lax.dot_general(selb, c2, _NT, preferred_element_type=F32)
                + lax.dot_general(selb, c3, _NT, preferred_element_type=F32))
        s = lax.dot_general(qbd, kn_ref[...], _NT, preferred_element_type=F32) + rowc - colc
        tq = lax.broadcasted_iota(jnp.int32, (rows, n_new), 0) % n_new
        tk = lax.broadcasted_iota(jnp.int32, (rows, n_new), 1)
        s = jnp.where(tk <= tq, s, NEG_INF)
        m = jnp.max(s, axis=-1, keepdims=True)
        p = jnp.exp(s - m)
        m_scr[...] = jnp.broadcast_to(m, m_scr.shape)
        l_scr[...] = jnp.broadcast_to(jnp.sum(p, axis=-1, keepdims=True), l_scr.shape)
        acc_scr[...] = _dot(p.astype(BF16), vn_ref[...])
        carry_scr[...] = jnp.zeros_like(carry_scr)

    hi, lo = _split2(jnp.concatenate([lf_refs[j][...] for j in range(g)], axis=0))
    within = _dot(hi, tri_ref[...]) + _dot(lo, tri_ref[...])
    tot = _dot(hi, ones_ref[...]) + _dot(lo, ones_ref[...])
    qbd = qbd_scr[...]
    rowc = rowc_scr[:, 0:1]
    carry = carry_scr[...]
    scores = []
    for j in range(g):
        kt = k_refs[j][...].reshape(w, page).astype(BF16)
        bias = within[j * heads:(j + 1) * heads, :] + carry
        carry = carry + tot[j * heads:(j + 1) * heads, :]
        bias_rows = jnp.concatenate(
            [jnp.broadcast_to(bias[h:h + 1, :], (n_new, page)) for h in range(heads)], axis=0)
        scores.append(_dot(qbd, kt) + rowc + bias_rows)
    carry_scr[...] = carry
    s = jnp.concatenate(scores, axis=1)
    m_old = m_scr[:, 0:1]
    m_new = jnp.maximum(m_old, jnp.max(s, axis=-1, keepdims=True))
    p = jnp.exp(s - m_new)

    @pl.when(jnp.max(p) > 0.0)
    def _():
        copies = []
        for j in range(g):
            page_id = pt_ref[b * n_pages + n_pages - 1 - (c * g + j)]
            copies.append(pltpu.make_async_copy(v_hbm.at[layer, page_id], v_buf.at[j], v_sem.at[j]))
            copies[j].start()
        alpha = jnp.exp(m_old - m_new)
        l_scr[...] = jnp.broadcast_to(alpha * l_scr[:, 0:1] + jnp.sum(p, axis=-1, keepdims=True), l_scr.shape)
        m_scr[...] = jnp.broadcast_to(m_new, m_scr.shape)
        pb = p.astype(BF16)
        acc = acc_scr[...] * alpha
        for j in range(g):
            copies[j].wait()
            vt = v_buf[j].reshape(w, page).astype(BF16)
            acc = acc + lax.dot_general(pb[:, j * page:(j + 1) * page], vt, _NT, preferred_element_type=F32)
        acc_scr[...] = acc

    @pl.when(c == n_chunks - 1)
    def _():
        full = jnp.where(own, acc_scr[...] / l_scr[:, 0:1], 0.0)
        out = full[0:n_new, :]
        for h in range(1, heads):
            out = out + full[h * n_new:(h + 1) * n_new, :]
        o_ref[...] = out.astype(BF16)


def _fox_paged(qn, kb, vb, c_new, kt_cache, vt_cache, lf_cache, layer, page_table, n_new):
    bsz, n_pages = page_table.shape
    _, n_pool, heads, hd, page = kt_cache.shape
    w = heads * hd
    g = _pick(n_pages, (16, 8, 4, 2, 1))
    n_chunks = n_pages // g
    rows = heads * n_new
    pt = page_table.reshape(-1)
    pos = jnp.arange(page)
    tri = (pos[:, None] > pos[None, :]).astype(BF16)
    ones = jnp.ones((page, page), BF16)

    def page_of(b, c, pt_ref, j):
        return pt_ref[b * n_pages + n_pages - 1 - (c * g + j)]

    tok = lambda b, c, pt_ref: (b, 0)
    const = lambda b, c, pt_ref: (0, 0)
    in_specs = [pl.BlockSpec((n_new, w), tok), pl.BlockSpec((n_new, w), tok), pl.BlockSpec((n_new, w), tok),
                pl.BlockSpec((n_new, LANES), tok),
                pl.BlockSpec((page, page), const), pl.BlockSpec((page, page), const)]
    in_specs += [pl.BlockSpec((None, None, heads, hd, page),
                              functools.partial(lambda b, c, pt_ref, j: (layer, page_of(b, c, pt_ref, j), 0, 0, 0), j=j))
                 for j in range(g)]
    in_specs += [pl.BlockSpec((None, None, heads, page),
                              functools.partial(lambda b, c, pt_ref, j: (layer, page_of(b, c, pt_ref, j), 0, 0), j=j))
                 for j in range(g)]
    in_specs += [pl.BlockSpec(memory_space=pl.ANY)]
    kern = functools.partial(_fox_paged_kernel, g=g, heads=heads, hd=hd, n_new=n_new, n_chunks=n_chunks,
                             n_pages=n_pages, layer=layer)
    grid_spec = pltpu.PrefetchScalarGridSpec(
        num_scalar_prefetch=1,
        grid=(bsz, n_chunks),
        in_specs=in_specs,
        out_specs=pl.BlockSpec((n_new, w), tok),
        scratch_shapes=[pltpu.VMEM((rows, LANES), F32), pltpu.VMEM((rows, LANES), F32),
                        pltpu.VMEM((rows, w), F32), pltpu.VMEM((heads, page), F32),
                        pltpu.VMEM((rows, w), BF16), pltpu.VMEM((rows, LANES), F32),
                        pltpu.VMEM((g, heads, hd, page), F32), pltpu.SemaphoreType.DMA((g,))],
    )
    return pl.pallas_call(
        kern,
        grid_spec=grid_spec,
        out_shape=jax.ShapeDtypeStruct((bsz * n_new, w), BF16),
        compiler_params=_params(2),
        name="fox_paged",
    )(pt, qn, kb, vb, c_new, tri, ones, *([kt_cache] * g), *([lf_cache] * g), vt_cache)


def _merge_kernel(x_ref, ca_ref, ra_ref, fo_ref, g0_ref, g1_ref, g2_ref, wpw_ref, bpw_ref, wro_ref, wfo_ref,
                  wout_ref, o_ref):
    y_conv = _dot(ca_ref[...], wpw_ref[...]) + bpw_ref[...]
    y_ret = _dot(ra_ref[...], wro_ref[...])
    y_fox = _dot(fo_ref[...], wfo_ref[...])
    merged = (_sigmoid(g0_ref[...].astype(F32)) * y_conv + _sigmoid(g1_ref[...].astype(F32)) * y_ret
              + _sigmoid(g2_ref[...].astype(F32)) * y_fox)
    o_ref[...] = x_ref[...] + _dot(merged.astype(BF16), wout_ref[...])


def _merge(x, z, ca, ra, fo, wpw, bpw, wro, wfo, wout, gate_col0):
    t, d = x.shape
    cw = ca.shape[1]
    tm = _pick(t, (512, 256))
    gc = gate_col0 // d
    row = lambda i: (i, 0)
    full = lambda i: (0, 0)
    return pl.pallas_call(
        _merge_kernel,
        grid=(t // tm,),
        in_specs=[
            pl.BlockSpec((tm, d), row),
            pl.BlockSpec((tm, cw), row), pl.BlockSpec((tm, cw), row), pl.BlockSpec((tm, cw), row),
            pl.BlockSpec((tm, d), lambda i: (i, gc)),
            pl.BlockSpec((tm, d), lambda i: (i, gc + 1)),
            pl.BlockSpec((tm, d), lambda i: (i, gc + 2)),
            pl.BlockSpec((cw, d), full), pl.BlockSpec((1, d), full),
            pl.BlockSpec((cw, d), full), pl.BlockSpec((cw, d), full), pl.BlockSpec((d, d), full),
        ],
        out_specs=pl.BlockSpec((tm, d), row),
        out_shape=jax.ShapeDtypeStruct((t, d), F32),
        compiler_params=_params(1),
        name="merge",
    )(x, ca, ra, fo, z, z, z, wpw, bpw, wro, wfo, wout)


def _mem_kv_kernel(m_ref, g_ref, wk_ref, wv_ref, gk_ref, k_ref, v_ref, *, heads, hd):
    mb = _rms(m_ref[...], g_ref[...]).astype(BF16)
    k = _dot(mb, wk_ref[...])
    k_ref[...] = jnp.concatenate(
        [_rms(k[:, h * hd:(h + 1) * hd], gk_ref[...]) for h in range(heads)], axis=1)
    v_ref[...] = _dot(mb, wv_ref[...])


def _mem_kv(mem, g, wk, wv, gk, heads, hd):
    t, d = mem.shape
    w = heads * hd
    tm = _pick(t, (512, 256, 128))
    full = lambda i: (0, 0)
    return pl.pallas_call(
        functools.partial(_mem_kv_kernel, heads=heads, hd=hd),
        grid=(t // tm,),
        in_specs=[pl.BlockSpec((tm, d), lambda i: (i, 0)), pl.BlockSpec((1, d), full),
                  pl.BlockSpec((d, w), full), pl.BlockSpec((d, w), full), pl.BlockSpec((1, hd), full)],
        out_specs=[pl.BlockSpec((tm, w), lambda i: (i, 0))] * 2,
        out_shape=[jax.ShapeDtypeStruct((t, w), F32)] * 2,
        compiler_params=_params(1),
        name="mem_kv",
    )(mem, g, wk, wv, gk)


def _cross_kernel(x_ref, gx_ref, wq_ref, gq_ref, mk_ref, mv_ref, wo_ref, o_ref, *, heads, hd, by_head):
    x = x_ref[...]
    tm = x.shape[0]
    q = _dot(_rms(x, gx_ref[...]).astype(BF16), wq_ref[...])
    qn = [_rms(q[:, h * hd:(h + 1) * hd], gq_ref[...]).astype(BF16) for h in range(heads)]
    if by_head:
        qs = jnp.concatenate(qn, axis=0)
        s = lax.dot_general(qs, mk_ref[...].astype(BF16), _NT, preferred_element_type=F32) * (hd ** -0.5)
        own = (lax.broadcasted_iota(jnp.int32, s.shape, 1) % heads
               == lax.broadcasted_iota(jnp.int32, s.shape, 0) // tm)
        s = jnp.where(own, s, NEG_INF)
        p = jnp.exp(s - jnp.max(s, axis=-1, keepdims=True))
        p = p / jnp.sum(p, axis=-1, keepdims=True)
        rows = _dot(p.astype(BF16), mv_ref[...].astype(BF16))
        outs = [rows[h * tm:(h + 1) * tm, :] for h in range(heads)]
    else:
        outs = []
        for h in range(heads):
            mk = mk_ref[:, h * hd:(h + 1) * hd].astype(BF16)
            mv = mv_ref[:, h * hd:(h + 1) * hd].astype(BF16)
            s = lax.dot_general(qn[h], mk, _NT, preferred_element_type=F32) * (hd ** -0.5)
            p = jnp.exp(s - jnp.max(s, axis=-1, keepdims=True))
            p = p / jnp.sum(p, axis=-1, keepdims=True)
            outs.append(_dot(p.astype(BF16), mv))
    o = jnp.concatenate(outs, axis=1).astype(BF16)
    o_ref[...] = x + _dot(o, wo_ref[...])


def _cross(x, gx, wq, gq, mem_k, mem_v, layer, wo, bsz, seq, heads, hd):
    t, d = x.shape
    w = heads * hd
    tm = _pick(seq, (512, 256, 128, 64, 32, 16, 8))
    n = seq // tm
    full = lambda b, i: (0, 0)
    by_head = mem_k.ndim == 5
    if by_head:
        depth, _, n_mem = mem_k.shape[:3]
        mem_k = mem_k.reshape(depth, bsz, n_mem * heads, hd)
        mem_v = mem_v.reshape(depth, bsz, n_mem * heads, hd)
        mem_spec = pl.BlockSpec((None, None, n_mem * heads, hd), lambda b, i: (layer, b, 0, 0))
    else:
        n_mem = mem_k.shape[1]
        mem_spec = pl.BlockSpec((None, n_mem, w), lambda b, i: (b, 0, 0))
    return pl.pallas_call(
        functools.partial(_cross_kernel, heads=heads, hd=hd, by_head=by_head),
        grid=(bsz, n),
        in_specs=[pl.BlockSpec((tm, d), lambda b, i: (b * n + i, 0)), pl.BlockSpec((1, d), full),
                  pl.BlockSpec((d, w), full), pl.BlockSpec((1, hd), full),
                  mem_spec, mem_spec,
                  pl.BlockSpec((w, d), full)],
        out_specs=pl.BlockSpec((tm, d), lambda b, i: (b * n + i, 0)),
        out_shape=jax.ShapeDtypeStruct((t, d), F32),
        compiler_params=_params(2),
        name="cross_attn",
    )(x, gx, wq, gq, mem_k, mem_v, wo)


def _ffn_kernel(x_ref, g_ref, w1_ref, w2_ref, o_ref, h_scr, acc_scr):
    k = pl.program_id(1)

    @pl.when(k == 0)
    def _():
        h_scr[...] = _rms(x_ref[...], g_ref[...]).astype(BF16)
        acc_scr[...] = jnp.zeros_like(acc_scr)

    a = jnp.maximum(_dot(h_scr[...], w1_ref[...]), 0.0)
    acc_scr[...] += _dot((a * a).astype(BF16), w2_ref[...])

    @pl.when(k == pl.num_programs(1) - 1)
    def _():
        o_ref[...] = x_ref[...] + acc_scr[...]


def _ffn(x, g, w1, w2):
    t, d = x.shape
    dff = w1.shape[1]
    tm = _pick(t, (1024, 512, 256))
    tf = 1024
    return pl.pallas_call(
        _ffn_kernel,
        grid=(t // tm, dff // tf),
        in_specs=[pl.BlockSpec((tm, d), lambda i, k: (i, 0)), pl.BlockSpec((1, d), lambda i, k: (0, 0)),
                  pl.BlockSpec((d, tf), lambda i, k: (0, k)), pl.BlockSpec((tf, d), lambda i, k: (k, 0))],
        out_specs=pl.BlockSpec((tm, d), lambda i, k: (i, 0)),
        out_shape=jax.ShapeDtypeStruct((t, d), F32),
        scratch_shapes=[pltpu.VMEM((tm, d), BF16), pltpu.VMEM((tm, d), F32)],
        compiler_params=_params(2),
        name="ffn",
    )(x, g, w1, w2)


def _layer(x, bsz, seq, conv_hist, ret_state0, ret_tables, mem_k, mem_v, layer, lw, dims, fox):
    d_conv, ret_heads, ret_dk, ret_dv, fox_heads, fox_hd, x_heads, x_hd = dims
    prompt = fox[0] == "prompt"
    z_dtype = BF16 if seq % BF16_ROWS == 0 else F32
    z, lf = _in_proj(x, lw["g_mix"], lw["w_main"], lw["b_main"], lw["w_f"], lw["b_f"], z_dtype)
    c_act, new_buf = _conv_branch(z, conv_hist, lw["conv_w"], lw["conv_b"], lw["conv_ln_g"], lw["conv_ln_b"],
                                  bsz, seq)
    ret_col0 = 2 * d_conv
    r_act, new_state = _ret_branch(z, ret_tables, lw["ret_gn"], ret_state0, bsz, seq, ret_col0)
    fox_col0 = ret_col0 + 2 * ret_heads * ret_dk + 2 * ret_heads * ret_dv
    if prompt:
        _, kbuf, vbuf = fox
        qw, kw, vw, kbuf, vbuf = _fox_prep_prompt(z, lf, lw["fox_g_q"], lw["fox_g_k"], kbuf, vbuf, layer,
                                                  bsz, seq, fox_col0, fox_heads, fox_hd)
        fo = _fox_flash(qw, kw, vw, bsz, seq, fox_heads, fox_hd, FLASH_HEADS)
        fox_out = (kbuf, vbuf)
    else:
        _, kt_cache, vt_cache, lf_cache, page_table = fox
        qn, kf, kb, vf, vb, c = _fox_prep(z, lf, lw["fox_g_q"], lw["fox_g_k"], bsz, seq, fox_col0,
                                          fox_heads, fox_hd)
        fo = _fox_paged(qn, kb, vb, c, kt_cache, vt_cache, lf_cache, layer, page_table, seq)
        fox_out = (kf, vf)
    gate_col0 = fox_col0 + 3 * fox_heads * fox_hd
    x = _merge(x, z, c_act, r_act, fo, lw["conv_w_pw"], lw["conv_b_pw"], lw["ret_w_o"], lw["fox_w_o"],
               lw["w_out"], gate_col0)
    x = _cross(x, lw["g_x"], lw["x_w_q"], lw["x_g_q"], mem_k, mem_v, layer, lw["x_w_o"], bsz, seq, x_heads, x_hd)
    x = _ffn(x, lw["g_ff"], lw["w_ff1"], lw["w_ff2"])
    return x, new_buf, new_state, fox_out, lf[:, :fox_heads]


def kernel(x_prompt, x_sample, cache_conv, state_ret, cache_fox_k, cache_fox_v, cache_fox_logf, cache_mem_k, cache_mem_v, page_table, mem_prompt, g_mix, w_in, b_in, conv_w, conv_b, conv_ln_g, conv_ln_b, conv_w_pw, conv_b_pw, ret_gn, ret_w_o, fox_g_q, fox_g_k, fox_w_o, w_out, g_x, g_mem, x_w_q, x_w_k, x_w_v, x_g_q, x_g_k, x_w_o, g_ff, w_ff1, w_ff2):
    bp, sp, d = x_prompt.shape
    bs, ss, _ = x_sample.shape
    depth = w_in.shape[0]
    width, d_conv = conv_w.shape[1], conv_w.shape[2]
    ret_heads, ret_dk, ret_dv = state_ret.shape[2], state_ret.shape[3], state_ret.shape[4]
    n_pool, page, fox_heads, fox_hd = cache_fox_k.shape[1:]
    n_mem, x_heads, x_hd = cache_mem_k.shape[2:]
    n_pages = page_table.shape[1]
    past = n_pages * page
    dims = (d_conv, ret_heads, ret_dk, ret_dv, fox_heads, fox_hd, x_heads, x_hd)
    fw = fox_heads * fox_hd
    forget_col = 2 * d_conv + 2 * ret_heads * ret_dk + 2 * ret_heads * ret_dv + 3 * fw

    tables_p = _ret_tables(sp, 0, ret_heads, ret_dk, ret_dv)
    tables_s = _ret_tables(ss, past, ret_heads, ret_dk, ret_dv)
    conv0 = jnp.zeros((bp, width - 1, d_conv), F32)
    ret0 = jnp.zeros((bp, ret_heads, ret_dk, ret_dv), F32)
    kt_cache = jnp.transpose(cache_fox_k, (0, 1, 3, 4, 2))
    vt_cache = jnp.transpose(cache_fox_v, (0, 1, 3, 4, 2))
    lf_cache = jnp.transpose(cache_fox_logf, (0, 1, 3, 2))
    kbuf = jnp.zeros((depth, bp, fw, sp), F32)
    vbuf = jnp.zeros((depth, bp, fw, sp), F32)

    xp = x_prompt.reshape(bp * sp, d)
    xs = x_sample.reshape(bs * ss, d)
    mem2d = mem_prompt.reshape(bp * n_mem, d)
    outs_p = [[] for _ in range(5)]
    outs_s = [[] for _ in range(5)]
    row = lambda a: a.reshape(1, -1)
    for l in range(depth):
        w_f = jnp.pad(w_in[l][:, forget_col:forget_col + fox_heads], ((0, 0), (0, LANES - fox_heads)))
        b_f = jnp.pad(b_in[l][forget_col:forget_col + fox_heads], (0, LANES - fox_heads))
        lw = {
            "g_mix": row(g_mix[l]),
            "w_main": jnp.concatenate([w_in[l][:, :forget_col], w_in[l][:, forget_col + fox_heads:]],
                                      axis=1).astype(BF16),
            "b_main": row(jnp.concatenate([b_in[l][:forget_col], b_in[l][forget_col + fox_heads:]])),
            "w_f": w_f.astype(BF16), "b_f": row(b_f),
            "conv_w": conv_w[l], "conv_b": row(conv_b[l]),
            "conv_ln_g": row(conv_ln_g[l]), "conv_ln_b": row(conv_ln_b[l]),
            "conv_w_pw": conv_w_pw[l].astype(BF16), "conv_b_pw": row(conv_b_pw[l]),
            "ret_gn": row(ret_gn[l]), "ret_w_o": ret_w_o[l].astype(BF16),
            "fox_g_q": row(jnp.tile(fox_g_q[l], fox_heads)), "fox_g_k": row(jnp.tile(fox_g_k[l], fox_heads)),
            "fox_w_o": fox_w_o[l].astype(BF16), "w_out": w_out[l].astype(BF16),
            "g_x": row(g_x[l]), "x_w_q": x_w_q[l].astype(BF16), "x_g_q": row(x_g_q[l]),
            "x_w_o": x_w_o[l].astype(BF16), "g_ff": row(g_ff[l]),
            "w_ff1": w_ff1[l].astype(BF16), "w_ff2": w_ff2[l].astype(BF16),
        }
        mk, mv = _mem_kv(mem2d, row(g_mem[l]), x_w_k[l].astype(BF16), x_w_v[l].astype(BF16), row(x_g_k[l]),
                         x_heads, x_hd)
        mk3 = mk.reshape(bp, n_mem, x_heads * x_hd)
        mv3 = mv.reshape(bp, n_mem, x_heads * x_hd)
        xp, c_b, r_s, (kbuf, vbuf), l_f = _layer(xp, bp, sp, conv0, ret0, tables_p, mk3, mv3, l, lw, dims,
                                                 ("prompt", kbuf, vbuf))
        for lst, val in zip(outs_p, (c_b, r_s, l_f.reshape(bp, sp, fox_heads),
                                     mk.reshape(bp, n_mem, x_heads, x_hd), mv.reshape(bp, n_mem, x_heads, x_hd))):
            lst.append(val)
        xs, c_b, r_s, (f_k, f_v), l_f = _layer(xs, bs, ss, cache_conv[l], state_ret[l], tables_s,
                                               cache_mem_k, cache_mem_v, l, lw, dims,
                                               ("sample", kt_cache, vt_cache, lf_cache, page_table))
        for lst, val in zip(outs_s, (c_b, r_s, f_k.reshape(bs, ss, fox_heads, fox_hd),
                                     f_v.reshape(bs, ss, fox_heads, fox_hd), l_f.reshape(bs, ss, fox_heads))):
            lst.append(val)
    conv_p, ret_p, lf_p, mk_p, mv_p = [jnp.stack(o) for o in outs_p]
    fk_p = jnp.transpose(kbuf.reshape(depth, bp, fox_heads, fox_hd, sp), (0, 1, 4, 2, 3))
    fv_p = jnp.transpose(vbuf.reshape(depth, bp, fox_heads, fox_hd, sp), (0, 1, 4, 2, 3))
    return (xp.reshape(bp, sp, d), xs.reshape(bs, ss, d), conv_p, ret_p, fk_p, fv_p, lf_p, mk_p, mv_p,
            *[jnp.stack(o) for o in outs_s])
```

```python
import functools
import math

import jax
import jax.numpy as jnp
from jax import lax
from jax.experimental import pallas as pl
from jax.experimental.pallas import tpu as pltpu

F32 = jnp.float32
BF16 = jnp.bfloat16
EPS = 1e-6
ROPE_BASE = 10000.0
RET_CHUNK = 256
LOG2E = 1.4426950408889634
LANES = 128
SUBLANES = 8
BF16_ROWS = 16
VMEM_LIMIT = 56 * 1024 * 1024
NEG_INF = float("-inf")
FLASH_HEADS = 4

_NT = (((1,), (1,)), ((), ()))


def _params(n_grid):
    return pltpu.CompilerParams(dimension_semantics=("arbitrary",) * n_grid,
                                vmem_limit_bytes=VMEM_LIMIT)


def _split2(x):
    hi = x.astype(BF16)
    lo = (x - hi.astype(F32)).astype(BF16)
    return hi, lo


def _split3(x):
    hi = x.astype(BF16)
    r = x - hi.astype(F32)
    mid = r.astype(BF16)
    lo = (r - mid.astype(F32)).astype(BF16)
    return hi, mid, lo


def _dot(a, b):
    return jnp.dot(a, b, preferred_element_type=F32)


def _rms(x, g):
    return x * lax.rsqrt(jnp.mean(x * x, axis=-1, keepdims=True) + EPS) * g


def _log_sigmoid(x):
    return jnp.minimum(x, 0.0) - jnp.log1p(jnp.exp(-jnp.abs(x)))


def _sigmoid(x):
    return 1.0 / (1.0 + jnp.exp(-x))


def _pick(n, cands):
    for c in cands:
        if n % c == 0:
            return c
    return n


def _head_norm(x, g, gm, hd, two_piece=True):
    if two_piece:
        hi, lo = _split2(x * x)
        ms = (_dot(hi, gm) + _dot(lo, gm)) * (1.0 / hd)
    else:
        ms = _dot((x * x).astype(BF16), gm) * (1.0 / hd)
    return x * lax.rsqrt(ms + EPS) * g


def _cumsum_rows(x, tri):
    h1, h2, h3 = _split3(x)
    return _dot(tri, h1) + _dot(tri, h2) + _dot(tri, h3)


def _in_proj_kernel(x_ref, g_ref, w_ref, b_ref, wf_ref, bf_ref, z_ref, lf_ref, h_scr):
    @pl.when(pl.program_id(1) == 0)
    def _():
        hb = _rms(x_ref[...], g_ref[...]).astype(BF16)
        h_scr[...] = hb
        lf_ref[...] = _log_sigmoid(_dot(hb, wf_ref[...]) + bf_ref[...])

    z_ref[...] = (_dot(h_scr[...], w_ref[...]) + b_ref[...]).astype(z_ref.dtype)


def _in_proj(x, g, w, b, wf, bf, z_dtype):
    t, d = x.shape
    n = w.shape[1]
    tm = _pick(t, (1024, 512, 256))
    tn = _pick(n, (1792, 1024))
    return pl.pallas_call(
        _in_proj_kernel,
        grid=(t // tm, n // tn),
        in_specs=[
            pl.BlockSpec((tm, d), lambda i, j: (i, 0)),
            pl.BlockSpec((1, d), lambda i, j: (0, 0)),
            pl.BlockSpec((d, tn), lambda i, j: (0, j)),
            pl.BlockSpec((1, tn), lambda i, j: (0, j)),
            pl.BlockSpec((d, LANES), lambda i, j: (0, 0)),
            pl.BlockSpec((1, LANES), lambda i, j: (0, 0)),
        ],
        out_specs=[
            pl.BlockSpec((tm, tn), lambda i, j: (i, j)),
            pl.BlockSpec((tm, LANES), lambda i, j: (i, 0)),
        ],
        out_shape=[jax.ShapeDtypeStruct((t, n), z_dtype), jax.ShapeDtypeStruct((t, LANES), F32)],
        scratch_shapes=[pltpu.VMEM((tm, d), BF16)],
        compiler_params=_params(2),
        name="in_proj",
    )(x, g, w, b, wf, bf)


def _conv_kernel(a_ref, b_ref, hist_ref, w_ref, cb_ref, lg_ref, lb_ref, shift_ref, act_ref, buf_ref, ext_scr,
                 *, tt, n_tiles, width, sub, pad):
    i = pl.program_id(1)
    hist = width - 1
    off = pad - hist
    kwin = sub + pad

    @pl.when(i == 0)
    def _():
        ext_scr[0:pad, :] = jnp.zeros((pad, ext_scr.shape[1]), F32)
        ext_scr[off:pad, :] = hist_ref[0]

    ext_scr[pad:pad + tt, :] = a_ref[...].astype(F32) * _sigmoid(b_ref[...].astype(F32))
    for r0 in range(0, tt, sub):
        win = ext_scr[r0:r0 + kwin, :]
        hi, lo = _split2(win)
        shifted = _dot(shift_ref[...], hi) + _dot(shift_ref[...], lo)
        parts = []
        for cs in (slice(c0, c0 + LANES) for c0 in range(0, win.shape[1], LANES)):
            acc = None
            for j in range(width):
                r, base = (off + j) % SUBLANES, (off + j) // SUBLANES * SUBLANES
                if r == 0:
                    rows = win[base:base + sub, cs]
                else:
                    rows = shifted[(r - 1) * kwin + base:(r - 1) * kwin + base + sub, cs]
                term = rows * w_ref[j:j + 1, cs]
                acc = term if acc is None else acc + term
            parts.append(acc)
        y = jnp.concatenate(parts, axis=1) + cb_ref[...]
        yc = y - jnp.mean(y, axis=-1, keepdims=True)
        yn = yc * lax.rsqrt(jnp.mean(yc * yc, axis=-1, keepdims=True) + EPS) * lg_ref[...] + lb_ref[...]
        act_ref[r0:r0 + sub, :] = (yn * _sigmoid(yn)).astype(BF16)

    @pl.when(i == n_tiles - 1)
    def _():
        buf_ref[0] = ext_scr[tt + off:tt + pad, :]

    if n_tiles > 1:
        ext_scr[off:pad, :] = ext_scr[tt + off:tt + pad, :]


def _conv_branch(z, hist, w, cb, lg, lb, bsz, seq):
    c = w.shape[1]
    width = w.shape[0]
    tt = _pick(seq, (256, 128, 64, 32, 16, 8))
    n_tiles = seq // tt
    sub = min(tt, 128)
    pad = 32
    assert width - 1 <= pad and (n_tiles == 1 or tt >= width - 1)
    kwin = sub + pad
    src = jnp.arange(kwin)
    shift = jnp.concatenate(
        [((src[:, None] + r == src[None, :]) & (src[:, None] + r < kwin)) for r in range(1, SUBLANES)],
        axis=0).astype(BF16)
    kern = functools.partial(_conv_kernel, tt=tt, n_tiles=n_tiles, width=width, sub=sub, pad=pad)
    return pl.pallas_call(
        kern,
        grid=(bsz, n_tiles),
        in_specs=[
            pl.BlockSpec((tt, c), lambda b, i: (b * n_tiles + i, 0)),
            pl.BlockSpec((tt, c), lambda b, i: (b * n_tiles + i, 1)),
            pl.BlockSpec((1, width - 1, c), lambda b, i: (b, 0, 0)),
            pl.BlockSpec((width, c), lambda b, i: (0, 0)),
            pl.BlockSpec((1, c), lambda b, i: (0, 0)),
            pl.BlockSpec((1, c), lambda b, i: (0, 0)),
            pl.BlockSpec((1, c), lambda b, i: (0, 0)),
            pl.BlockSpec(((SUBLANES - 1) * kwin, kwin), lambda b, i: (0, 0)),
        ],
        out_specs=[
            pl.BlockSpec((tt, c), lambda b, i: (b * n_tiles + i, 0)),
            pl.BlockSpec((1, width - 1, c), lambda b, i: (b, 0, 0)),
        ],
        out_shape=[jax.ShapeDtypeStruct((bsz * seq, c), BF16),
                   jax.ShapeDtypeStruct((bsz, width - 1, c), F32)],
        scratch_shapes=[pltpu.VMEM((pad + tt, c), F32)],
        compiler_params=_params(2),
        name="conv_branch",
    )(z, z, hist, w, cb, lg, lb, shift)


def _ret_kernel(q_ref, k_ref, v_ref, rg_ref, cos_ref, sin_ref, dm_ref, qd_ref, kd_ref, cd_ref, gn_ref,
                s0_ref, act_ref, s_ref, *, heads, dk, dv, group):
    @pl.when(pl.program_id(1) == 0)
    def _():
        s_ref[...] = s0_ref[...]

    cos = cos_ref[...]
    sin = sin_ref[...]
    half = dk // 2

    def rot(x):
        lane = lax.broadcasted_iota(jnp.int32, x.shape, 1)
        upper = (lane % dk) >= half
        swapped = jnp.where(upper, pltpu.roll(x, half, 1), pltpu.roll(x, x.shape[1] - half, 1))
        return x * cos + swapped * sin

    for bb in range(group):
        q = rot(q_ref[bb].astype(F32)) * (dk ** -0.5)
        k = rot(k_ref[bb].astype(F32))
        qb = q.astype(BF16)
        kb = k.astype(BF16)
        kdt = (k * kd_ref[...]).T.astype(BF16)
        vb = v_ref[bb].astype(BF16)
        outs = []
        for h in range(heads):
            qh = qb[:, h * dk:(h + 1) * dk]
            kh = kb[:, h * dk:(h + 1) * dk]
            vh = vb[:, h * dv:(h + 1) * dv]
            inner = lax.dot_general(qh, kh, _NT, preferred_element_type=F32) * dm_ref[h]
            s_old = s_ref[bb, h]
            o = _dot(inner.astype(BF16), vh) + _dot(qh, s_old.astype(BF16)) * qd_ref[:, h * dv:(h + 1) * dv]
            s_ref[bb, h] = s_old * cd_ref[h] + _dot(kdt[h * dk:(h + 1) * dk, :], vh)
            outs.append(_rms(o, gn_ref[:, h * dv:(h + 1) * dv]))
        o_all = jnp.concatenate(outs, axis=1)
        rg = rg_ref[bb].astype(F32)
        act_ref[bb] = (rg * _sigmoid(rg) * o_all).astype(BF16)


def _ret_tables(seq, pos0, heads, dk, dv):
    c = math.gcd(seq, RET_CHUNK)
    half = dk // 2
    inv_freq = jnp.exp(-math.log(ROPE_BASE) * jnp.arange(half, dtype=F32) / half)
    pos = (pos0 + jnp.arange(seq)).astype(F32)
    ang = pos[:, None] * inv_freq[None, :]
    cos = jnp.tile(jnp.concatenate([jnp.cos(ang), jnp.cos(ang)], axis=-1), (1, heads))
    sin = jnp.tile(jnp.concatenate([-jnp.sin(ang), jnp.sin(ang)], axis=-1), (1, heads))
    log_g = jnp.log1p(-jnp.exp2(-5.0 - jnp.arange(heads, dtype=F32)))
    idx = jnp.arange(c, dtype=F32)
    diff = idx[:, None] - idx[None, :]
    dmask = jnp.where(diff >= 0, jnp.exp(log_g[:, None, None] * jnp.maximum(diff, 0.0)), 0.0)
    q_dec = jnp.exp(log_g[:, None] * (idx[None, :] + 1.0))
    k_dec = jnp.exp(log_g[:, None] * (c - 1.0 - idx[None, :]))
    chunk_dec = jnp.exp(log_g * c)
    qd = jnp.repeat(q_dec.T, dv, axis=1)
    kd = jnp.repeat(k_dec.T, dk, axis=1)
    cd = jnp.broadcast_to(chunk_dec[:, None, None], (heads, 1, dv))
    return c, cos, sin, dmask, qd, kd, cd


def _ret_branch(z, tables, gn, state0, bsz, seq, col0):
    heads, dk, dv = state0.shape[1], state0.shape[2], state0.shape[3]
    c, cos, sin, dmask, qd, kd, cd = tables
    n = seq // c
    hk, hv = heads * dk, heads * dv
    qcol, kcol, vcol, gcol = col0 // hk, col0 // hk + 1, (col0 + 2 * hk) // hv, (col0 + 2 * hk) // hv + 1
    group = _pick(bsz, (4, 2, 1))
    z3 = z.reshape(bsz, seq, z.shape[1])
    kern = functools.partial(_ret_kernel, heads=heads, dk=dk, dv=dv, group=group)
    act, state = pl.pallas_call(
        kern,
        grid=(bsz // group, n),
        in_specs=[
            pl.BlockSpec((group, c, hk), lambda g, i: (g, i, qcol)),
            pl.BlockSpec((group, c, hk), lambda g, i: (g, i, kcol)),
            pl.BlockSpec((group, c, hv), lambda g, i: (g, i, vcol)),
            pl.BlockSpec((group, c, hv), lambda g, i: (g, i, gcol)),
            pl.BlockSpec((c, hk), lambda g, i: (i, 0)),
            pl.BlockSpec((c, hk), lambda g, i: (i, 0)),
            pl.BlockSpec((heads, c, c), lambda g, i: (0, 0, 0)),
            pl.BlockSpec((c, hv), lambda g, i: (0, 0)),
            pl.BlockSpec((c, hk), lambda g, i: (0, 0)),
            pl.BlockSpec((heads, 1, dv), lambda g, i: (0, 0, 0)),
            pl.BlockSpec((1, hv), lambda g, i: (0, 0)),
            pl.BlockSpec((group, heads, dk, dv), lambda g, i: (g, 0, 0, 0)),
        ],
        out_specs=[
            pl.BlockSpec((group, c, hv), lambda g, i: (g, i, 0)),
            pl.BlockSpec((group, heads, dk, dv), lambda g, i: (g, 0, 0, 0)),
        ],
        out_shape=[jax.ShapeDtypeStruct((bsz, seq, hv), BF16),
                   jax.ShapeDtypeStruct(state0.shape, F32)],
        compiler_params=_params(2),
        name="retention",
    )(z3, z3, z3, z3, cos, sin, dmask, qd, kd, cd, gn, state0)
    return act.reshape(bsz * seq, hv), state


def _fox_prep_kernel(fq_ref, fk_ref, fv_ref, lf_ref, gq_ref, gk_ref, gm_ref, tri_ref,
                     qn_ref, kf_ref, kb_ref, vf_ref, vb_ref, c_ref, carry, *, hd):
    @pl.when(pl.program_id(1) == 0)
    def _():
        carry[...] = jnp.zeros_like(carry)

    gm = gm_ref[...]
    qn_ref[...] = (_head_norm(fq_ref[...].astype(F32), gq_ref[...], gm, hd) * (hd ** -0.5)).astype(BF16)
    kn = _head_norm(fk_ref[...].astype(F32), gk_ref[...], gm, hd)
    kf_ref[...] = kn
    kb_ref[...] = kn.astype(BF16)
    v = fv_ref[...].astype(F32)
    vf_ref[...] = v
    vb_ref[...] = v.astype(BF16)
    cs = _cumsum_rows(lf_ref[...], tri_ref[...]) + carry[0:1, :]
    c_ref[...] = cs
    tm = cs.shape[0]
    carry[...] = jnp.broadcast_to(cs[tm - 1:tm, :], carry.shape)


def _fox_consts(heads, hd, tm):
    gm = jnp.kron(jnp.eye(heads, dtype=F32), jnp.ones((hd, hd), F32)).astype(BF16)
    tri = jnp.tril(jnp.ones((tm, tm), F32)).astype(BF16)
    return gm, tri


def _fox_prep(z, lf, gq, gk, bsz, seq, col0, heads, hd):
    t = bsz * seq
    w = heads * hd
    tm = _pick(seq, (512, 256, 128, 64, 32, 16, 8))
    n = seq // tm
    qcol = col0 // w
    gm, tri = _fox_consts(heads, hd, tm)
    row = lambda b, i: (b * n + i, 0)
    full = lambda b, i: (0, 0)
    return pl.pallas_call(
        functools.partial(_fox_prep_kernel, hd=hd),
        grid=(bsz, n),
        in_specs=[
            pl.BlockSpec((tm, w), lambda b, i: (b * n + i, qcol)),
            pl.BlockSpec((tm, w), lambda b, i: (b * n + i, qcol + 1)),
            pl.BlockSpec((tm, w), lambda b, i: (b * n + i, qcol + 2)),
            pl.BlockSpec((tm, LANES), row),
            pl.BlockSpec((1, w), full), pl.BlockSpec((1, w), full),
            pl.BlockSpec((w, w), full), pl.BlockSpec((tm, tm), full),
        ],
        out_specs=[pl.BlockSpec((tm, w), row)] * 5 + [pl.BlockSpec((tm, LANES), row)],
        out_shape=[jax.ShapeDtypeStruct((t, w), BF16), jax.ShapeDtypeStruct((t, w), F32),
                   jax.ShapeDtypeStruct((t, w), BF16), jax.ShapeDtypeStruct((t, w), F32),
                   jax.ShapeDtypeStruct((t, w), BF16), jax.ShapeDtypeStruct((t, LANES), F32)],
        scratch_shapes=[pltpu.VMEM((SUBLANES, LANES), F32)],
        compiler_params=_params(2),
        name="fox_prep",
    )(z, z, z, lf, gq, gk, gm, tri)


def _fox_prep_prompt_kernel(fq_ref, fk_ref, fv_ref, lf_ref, gq_ref, gk_ref, gm_ref, tri_ref,
                            cq_ref, ck_ref, rows_ref, kbuf_ref, vbuf_ref,
                            q_ref, k_ref, v_ref, kt_ref, vt_ref, carry, *, hd, heads):
    del kbuf_ref, vbuf_ref

    @pl.when(pl.program_id(1) == 0)
    def _():
        carry[...] = jnp.zeros_like(carry)

    gm = gm_ref[...]

    def spread(x):
        gap = jnp.zeros((x.shape[0], LANES - hd), x.dtype)
        return jnp.concatenate([piece for h in range(x.shape[1] // hd)
                                for piece in (x[:, h * hd:(h + 1) * hd], gap)], axis=1)

    qn = _head_norm(fq_ref[...].astype(F32), gq_ref[...], gm, hd, two_piece=False) * (LOG2E * hd ** -0.5)
    kn = _head_norm(fk_ref[...].astype(F32), gk_ref[...], gm, hd)
    v = fv_ref[...].astype(F32)
    cs = _cumsum_rows(lf_ref[...], tri_ref[...]) + carry[0:1, :]
    tm = cs.shape[0]
    carry[...] = jnp.broadcast_to(cs[tm - 1:tm, :], carry.shape)
    lane = lax.broadcasted_iota(jnp.int32, cs.shape, 1)
    c1, c2, c3 = _split3(jnp.where(lane < heads, cs * LOG2E, 0.0))
    pieces = (c1.astype(F32) + pltpu.roll(c2.astype(F32), heads, 1)
              + pltpu.roll(c3.astype(F32), 2 * heads, 1)).astype(BF16)
    q_ref[...] = (spread(qn) + _dot(pieces, cq_ref[...]) + rows_ref[0:1, :]).astype(BF16)
    k_ref[...] = (spread(kn) + _dot(pieces, ck_ref[...]) + rows_ref[1:2, :]).astype(BF16)
    v_ref[...] = (spread(v) + rows_ref[2:3, :]).astype(BF16)
    kt_ref[...] = kn.T
    vt_ref[...] = v.T


def _fox_prep_prompt(z, lf, gq, gk, kbuf, vbuf, layer, bsz, seq, col0, heads, hd):
    t = bsz * seq
    w = heads * hd
    wide = heads * LANES
    tm = _pick(seq, (512, 256, 128))
    n = seq // tm
    qcol = col0 // w
    gm, tri = _fox_consts(heads, hd, tm)
    src = jnp.arange(LANES)[:, None]
    dst = jnp.arange(wide)[None, :]
    assert 3 * heads <= LANES

    def lane_sel(first):
        return (dst == (src % heads) * LANES + first + src // heads) & (src < 3 * heads)

    cq = lane_sel(hd).astype(BF16)
    ck = -lane_sel(hd + 3).astype(BF16)
    lane_in_head = jnp.arange(wide) % LANES
    rows = jnp.stack([((lane_in_head >= hd + 3) & (lane_in_head < hd + 6)).astype(F32),
                      ((lane_in_head >= hd) & (lane_in_head < hd + 3)).astype(F32),
                      (lane_in_head == hd).astype(F32)] + [jnp.zeros((wide,), F32)] * (SUBLANES - 3))
    row = lambda b, i: (b * n + i, 0)
    full = lambda b, i: (0, 0)
    tspec = pl.BlockSpec((None, None, w, tm), lambda b, i: (layer, b, 0, i))
    return pl.pallas_call(
        functools.partial(_fox_prep_prompt_kernel, hd=hd, heads=heads),
        grid=(bsz, n),
        in_specs=[
            pl.BlockSpec((tm, w), lambda b, i: (b * n + i, qcol)),
            pl.BlockSpec((tm, w), lambda b, i: (b * n + i, qcol + 1)),
            pl.BlockSpec((tm, w), lambda b, i: (b * n + i, qcol + 2)),
            pl.BlockSpec((tm, LANES), row),
            pl.BlockSpec((1, w), full), pl.BlockSpec((1, w), full),
            pl.BlockSpec((w, w), full), pl.BlockSpec((tm, tm), full),
            pl.BlockSpec((LANES, wide), full), pl.BlockSpec((LANES, wide), full),
            pl.BlockSpec((SUBLANES, wide), full),
            pl.BlockSpec(memory_space=pl.ANY), pl.BlockSpec(memory_space=pl.ANY),
        ],
        out_specs=[pl.BlockSpec((tm, wide), row)] * 3 + [tspec, tspec],
        out_shape=[jax.ShapeDtypeStruct((t, wide), BF16)] * 3
        + [jax.ShapeDtypeStruct(kbuf.shape, F32), jax.ShapeDtypeStruct(vbuf.shape, F32)],
        input_output_aliases={11: 3, 12: 4},
        scratch_shapes=[pltpu.VMEM((SUBLANES, LANES), F32)],
        compiler_params=_params(2),
        name="fox_prep_prompt",
    )(z, z, z, lf, gq, gk, gm, tri, cq, ck, rows, kbuf, vbuf)


def _fox_flash_kernel(q_ref, k_ref, v_ref, o_ref, *, tq, tk, hd, nh):
    i = pl.program_id(2)
    r = tq // tk
    lane = lax.broadcasted_iota(jnp.int32, (tq, LANES), 1)

    def update(m_old, acc, s, v_blk):
        m_new = jnp.maximum(m_old, jnp.max(s, axis=-1, keepdims=True))
        p = jnp.exp2(s - m_new)
        return m_new, acc * jnp.exp2(m_old - m_new) + _dot(p.astype(BF16), v_blk)

    def full_block(j, carry):
        start = pl.multiple_of(j * tk, tk)
        out = []
        for e in range(nh):
            cols = slice(e * LANES, (e + 1) * LANES)
            s = lax.dot_general(q_ref[:, cols], k_ref[pl.ds(start, tk), cols], _NT, preferred_element_type=F32)
            out.append(update(*carry[e], s, v_ref[pl.ds(start, tk), cols]))
        return tuple(out)

    init = ((jnp.full((tq, 1), NEG_INF, F32), jnp.zeros((tq, LANES), F32)),) * nh
    carry = list(lax.fori_loop(0, i * r, full_block, init))
    visible = (lax.broadcasted_iota(jnp.int32, (tk, tk), 1) <= lax.broadcasted_iota(jnp.int32, (tk, tk), 0))
    for d in range(r):
        start = pl.multiple_of((i * r + d) * tk, tk)
        lo = d * tk
        for e in range(nh):
            cols = slice(e * LANES, (e + 1) * LANES)
            m_old, acc = carry[e]
            s = lax.dot_general(q_ref[lo:, cols], k_ref[pl.ds(start, tk), cols], _NT, preferred_element_type=F32)
            top = jnp.where(visible, s[:tk], NEG_INF)
            s = top if tq - lo == tk else jnp.concatenate([top, s[tk:]], axis=0)
            m_new, acc_new = update(m_old[lo:], acc[lo:], s, v_ref[pl.ds(start, tk), cols])
            if lo:
                m_new = jnp.concatenate([m_old[:lo], m_new], axis=0)
                acc_new = jnp.concatenate([acc[:lo], acc_new], axis=0)
            carry[e] = (m_new, acc_new)
    outs = []
    for e in range(nh):
        acc = carry[e][1]
        denom = jnp.sum(jnp.where(lane == hd, acc, 0.0), axis=-1, keepdims=True)
        outs.append(acc / denom)
    for e in range(0, nh, 2):
        o_ref[:, e // 2 * LANES:(e // 2 + 1) * LANES] = jnp.where(
            lane < hd, outs[e], pltpu.roll(outs[e + 1], hd, 1)).astype(BF16)


def _fox_flash(qw, kw, vw, bsz, seq, heads, hd, nh):
    t = bsz * seq
    tk = _pick(seq, (512, 256, 128))
    tq = _pick(seq, (2 * tk, tk))
    nq = seq // tq
    kern = functools.partial(_fox_flash_kernel, tq=tq, tk=tk, hd=hd, nh=nh)
    return pl.pallas_call(
        kern,
        grid=(bsz, heads // nh, nq),
        in_specs=[
            pl.BlockSpec((tq, nh * LANES), lambda b, h, i: (b * nq + i, h)),
            pl.BlockSpec((seq, nh * LANES), lambda b, h, i: (b, h)),
            pl.BlockSpec((seq, nh * LANES), lambda b, h, i: (b, h)),
        ],
        out_specs=pl.BlockSpec((tq, nh * hd), lambda b, h, i: (b * nq + i, h)),
        out_shape=jax.ShapeDtypeStruct((t, heads * hd), BF16),
        compiler_params=_params(3),
        name="fox_flash",
    )(qw, kw, vw)


def _fox_paged_kernel(pt_ref, q_ref, kn_ref, vn_ref, cn_ref, tri_ref, ones_ref, *refs,
                      g, heads, hd, n_new, n_chunks, n_pages, layer):
    k_refs = refs[0:g]
    lf_refs = refs[g:2 * g]
    v_hbm = refs[2 * g]
    o_ref, m_scr, l_scr, acc_scr, carry_scr, qbd_scr, rowc_scr, v_buf, v_sem = refs[2 * g + 1:]
    b = pl.program_id(0)
    c = pl.program_id(1)
    rows = heads * n_new
    w = heads * hd
    page = k_refs[0].shape[-1]
    own = (lax.broadcasted_iota(jnp.int32, (rows, w), 0) // n_new
           == lax.broadcasted_iota(jnp.int32, (rows, w), 1) // hd)

    @pl.when(c == 0)
    def _():
        q = q_ref[...].astype(F32)
        qbd = jnp.where(own, jnp.concatenate([q] * heads, axis=0), 0.0).astype(BF16)
        qbd_scr[...] = qbd
        cn = cn_ref[...]
        sel = (lax.broadcasted_iota(jnp.int32, (rows, LANES), 1)
               == lax.broadcasted_iota(jnp.int32, (rows, LANES), 0) // n_new)
        rowc = jnp.sum(jnp.where(sel, jnp.concatenate([cn] * heads, axis=0), 0.0), axis=-1, keepdims=True)
        rowc_scr[...] = jnp.broadcast_to(rowc, rowc_scr.shape)
        selb = jnp.where(sel, 1.0, 0.0).astype(BF16)
        c1, c2, c3 = _split3(cn)
        colc = (lax.dot_general(selb, c1, _NT, preferred_element_type=F32)
                + lax.dot_general(selb, c2, _NT, preferred_element_type=F32)
                + lax.dot_general(selb, c3, _NT, preferred_element_type=F32))
        s = lax.dot_general(qbd, kn_ref[...], _NT, preferred_element_type=F32) + rowc - colc
        tq = lax.broadcasted_iota(jnp.int32, (rows, n_new), 0) % n_new
        tk = lax.broadcasted_iota(jnp.int32, (rows, n_new), 1)
        s = jnp.where(tk <= tq, s, NEG_INF)
        m = jnp.max(s, axis=-1, keepdims=True)
        p = jnp.exp(s - m)
        m_scr[...] = jnp.broadcast_to(m, m_scr.shape)
        l_scr[...] = jnp.broadcast_to(jnp.sum(p, axis=-1, keepdims=True), l_scr.shape)
        acc_scr[...] = _dot(p.astype(BF16), vn_ref[...])
        carry_scr[...] = jnp.zeros_like(carry_scr)

    hi, lo = _split2(jnp.concatenate([lf_refs[j][...] for j in range(g)], axis=0))
    within = _dot(hi, tri_ref[...]) + _dot(lo, tri_ref[...])
    tot = _dot(hi, ones_ref[...]) + _dot(lo, ones_ref[...])
    qbd = qbd_scr[...]
    rowc = rowc_scr[:, 0:1]
    carry = carry_scr[...]
    scores = []
    for j in range(g):
        kt = k_refs[j][...].reshape(w, page).astype(BF16)
        bias = within[j * heads:(j + 1) * heads, :] + carry
        carry = carry + tot[j * heads:(j + 1) * heads, :]
        bias_rows = jnp.concatenate(
            [jnp.broadcast_to(bias[h:h + 1, :], (n_new, page)) for h in range(heads)], axis=0)
        scores.append(_dot(qbd, kt) + rowc + bias_rows)
    carry_scr[...] = carry
    s = jnp.concatenate(scores, axis=1)
    m_old = m_scr[:, 0:1]
    m_new = jnp.maximum(m_old, jnp.max(s, axis=-1, keepdims=True))
    p = jnp.exp(s - m_new)

    @pl.when(jnp.max(p) > 0.0)
    def _():
        copies = []
        for j in range(g):
            page_id = pt_ref[b * n_pages + n_pages - 1 - (c * g + j)]
            copies.append(pltpu.make_async_copy(v_hbm.at[layer, page_id], v_buf.at[j], v_sem.at[j]))
            copies[j].start()
        alpha = jnp.exp(m_old - m_new)
        l_scr[...] = jnp.broadcast_to(alpha * l_scr[:, 0:1] + jnp.sum(p, axis=-1, keepdims=True), l_scr.shape)
        m_scr[...] = jnp.broadcast_to(m_new, m_scr.shape)
        pb = p.astype(BF16)
        acc = acc_scr[...] * alpha
        for j in range(g):
            copies[j].wait()
            vt = v_buf[j].reshape(w, page).astype(BF16)
            acc = acc + lax.dot_general(pb[:, j * page:(j + 1) * page], vt, _NT, preferred_element_type=F32)
        acc_scr[...] = acc

    @pl.when(c == n_chunks - 1)
    def _():
        full = jnp.where(own, acc_scr[...] / l_scr[:, 0:1], 0.0)
        out = full[0:n_new, :]
        for h in range(1, heads):
            out = out + full[h * n_new:(h + 1) * n_new, :]
        o_ref[...] = out.astype(BF16)


def _fox_paged(qn, kb, vb, c_new, kt_cache, vt_cache, lf_cache, layer, page_table, n_new):
    bsz, n_pages = page_table.shape
    _, n_pool, heads, hd, page = kt_cache.shape
    w = heads * hd
    g = _pick(n_pages, (16, 8, 4, 2, 1))
    n_chunks = n_pages // g
    rows = heads * n_new
    pt = page_table.reshape(-1)
    pos = jnp.arange(page)
    tri = (pos[:, None] > pos[None, :]).astype(BF16)
    ones = jnp.ones((page, page), BF16)

    def page_of(b, c, pt_ref, j):
        return pt_ref[b * n_pages + n_pages - 1 - (c * g + j)]

    tok = lambda b, c, pt_ref: (b, 0)
    const = lambda b, c, pt_ref: (0, 0)
    in_specs = [pl.BlockSpec((n_new, w), tok), pl.BlockSpec((n_new, w), tok), pl.BlockSpec((n_new, w), tok),
                pl.BlockSpec((n_new, LANES), tok),
                pl.BlockSpec((page, page), const), pl.BlockSpec((page, page), const)]
    in_specs += [pl.BlockSpec((None, None, heads, hd, page),
                              functools.partial(lambda b, c, pt_ref, j: (layer, page_of(b, c, pt_ref, j), 0, 0, 0), j=j))
                 for j in range(g)]
    in_specs += [pl.BlockSpec((None, None, heads, page),
                              functools.partial(lambda b, c, pt_ref, j: (layer, page_of(b, c, pt_ref, j), 0, 0), j=j))
                 for j in range(g)]
    in_specs += [pl.BlockSpec(memory_space=pl.ANY)]
    kern = functools.partial(_fox_paged_kernel, g=g, heads=heads, hd=hd, n_new=n_new, n_chunks=n_chunks,
                             n_pages=n_pages, layer=layer)
    grid_spec = pltpu.PrefetchScalarGridSpec(
        num_scalar_prefetch=1,
        grid=(bsz, n_chunks),
        in_specs=in_specs,
        out_specs=pl.BlockSpec((n_new, w), tok),
        scratch_shapes=[pltpu.VMEM((rows, LANES), F32), pltpu.VMEM((rows, LANES), F32),
                        pltpu.VMEM((rows, w), F32), pltpu.VMEM((heads, page), F32),
                        pltpu.VMEM((rows, w), BF16), pltpu.VMEM((rows, LANES), F32),
                        pltpu.VMEM((g, heads, hd, page), F32), pltpu.SemaphoreType.DMA((g,))],
    )
    return pl.pallas_call(
        kern,
        grid_spec=grid_spec,
        out_shape=jax.ShapeDtypeStruct((bsz * n_new, w), BF16),
        compiler_params=_params(2),
        name="fox_paged",
    )(pt, qn, kb, vb, c_new, tri, ones, *([kt_cache] * g), *([lf_cache] * g), vt_cache)


def _merge_kernel(x_ref, ca_ref, ra_ref, fo_ref, g0_ref, g1_ref, g2_ref, wpw_ref, bpw_ref, wro_ref, wfo_ref,
                  wout_ref, o_ref):
    y_conv = _dot(ca_ref[...], wpw_ref[...]) + bpw_ref[...]
    y_ret = _dot(ra_ref[...], wro_ref[...])
    y_fox = _dot(fo_ref[...], wfo_ref[...])
    merged = (_sigmoid(g0_ref[...].astype(F32)) * y_conv + _sigmoid(g1_ref[...].astype(F32)) * y_ret
              + _sigmoid(g2_ref[...].astype(F32)) * y_fox)
    o_ref[...] = x_ref[...] + _dot(merged.astype(BF16), wout_ref[...])


def _merge(x, z, ca, ra, fo, wpw, bpw, wro, wfo, wout, gate_col0):
    t, d = x.shape
    cw = ca.shape[1]
    tm = _pick(t, (512, 256))
    gc = gate_col0 // d
    row = lambda i: (i, 0)
    full = lambda i: (0, 0)
    return pl.pallas_call(
        _merge_kernel,
        grid=(t // tm,),
        in_specs=[
            pl.BlockSpec((tm, d), row),
            pl.BlockSpec((tm, cw), row), pl.BlockSpec((tm, cw), row), pl.BlockSpec((tm, cw), row),
            pl.BlockSpec((tm, d), lambda i: (i, gc)),
            pl.BlockSpec((tm, d), lambda i: (i, gc + 1)),
            pl.BlockSpec((tm, d), lambda i: (i, gc + 2)),
            pl.BlockSpec((cw, d), full), pl.BlockSpec((1, d), full),
            pl.BlockSpec((cw, d), full), pl.BlockSpec((cw, d), full), pl.BlockSpec((d, d), full),
        ],
        out_specs=pl.BlockSpec((tm, d), row),
        out_shape=jax.ShapeDtypeStruct((t, d), F32),
        compiler_params=_params(1),
        name="merge",
    )(x, ca, ra, fo, z, z, z, wpw, bpw, wro, wfo, wout)


def _mem_kv_kernel(m_ref, g_ref, wk_ref, wv_ref, gk_ref, k_ref, v_ref, *, heads, hd):
    mb = _rms(m_ref[...], g_ref[...]).astype(BF16)
    k = _dot(mb, wk_ref[...])
    k_ref[...] = jnp.concatenate(
        [_rms(k[:, h * hd:(h + 1) * hd], gk_ref[...]) for h in range(heads)], axis=1)
    v_ref[...] = _dot(mb, wv_ref[...])


def _mem_kv(mem, g, wk, wv, gk, heads, hd):
    t, d = mem.shape
    w = heads * hd
    tm = _pick(t, (512, 256, 128))
    full = lambda i: (0, 0)
    return pl.pallas_call(
        functools.partial(_mem_kv_kernel, heads=heads, hd=hd),
        grid=(t // tm,),
        in_specs=[pl.BlockSpec((tm, d), lambda i: (i, 0)), pl.BlockSpec((1, d), full),
                  pl.BlockSpec((d, w), full), pl.BlockSpec((d, w), full), pl.BlockSpec((1, hd), full)],
        out_specs=[pl.BlockSpec((tm, w), lambda i: (i, 0))] * 2,
        out_shape=[jax.ShapeDtypeStruct((t, w), F32)] * 2,
        compiler_params=_params(1),
        name="mem_kv",
    )(mem, g, wk, wv, gk)


def _cross_kernel(x_ref, gx_ref, wq_ref, gq_ref, mk_ref, mv_ref, wo_ref, o_ref, *, heads, hd, by_head):
    x = x_ref[...]
    tm = x.shape[0]
    q = _dot(_rms(x, gx_ref[...]).astype(BF16), wq_ref[...])
    qn = [_rms(q[:, h * hd:(h + 1) * hd], gq_ref[...]).astype(BF16) for h in range(heads)]
    if by_head:
        qs = jnp.concatenate(qn, axis=0)
        s = lax.dot_general(qs, mk_ref[...].astype(BF16), _NT, preferred_element_type=F32) * (hd ** -0.5)
        own = (lax.broadcasted_iota(jnp.int32, s.shape, 1) % heads
               == lax.broadcasted_iota(jnp.int32, s.shape, 0) // tm)
        s = jnp.where(own, s, NEG_INF)
        p = jnp.exp(s - jnp.max(s, axis=-1, keepdims=True))
        p = p / jnp.sum(p, axis=-1, keepdims=True)
        rows = _dot(p.astype(BF16), mv_ref[...].astype(BF16))
        outs = [rows[h * tm:(h + 1) * tm, :] for h in range(heads)]
    else:
        outs = []
        for h in range(heads):
            mk = mk_ref[:, h * hd:(h + 1) * hd].astype(BF16)
            mv = mv_ref[:, h * hd:(h + 1) * hd].astype(BF16)
            s = lax.dot_general(qn[h], mk, _NT, preferred_element_type=F32) * (hd ** -0.5)
            p = jnp.exp(s - jnp.max(s, axis=-1, keepdims=True))
            p = p / jnp.sum(p, axis=-1, keepdims=True)
            outs.append(_dot(p.astype(BF16), mv))
    o = jnp.concatenate(outs, axis=1).astype(BF16)
    o_ref[...] = x + _dot(o, wo_ref[...])


def _cross(x, gx, wq, gq, mem_k, mem_v, layer, wo, bsz, seq, heads, hd):
    t, d = x.shape
    w = heads * hd
    tm = _pick(seq, (1024, 512, 256, 128, 64, 32, 16, 8))
    n = seq // tm
    full = lambda b, i: (0, 0)
    by_head = mem_k.ndim == 5
    if by_head:
        depth, _, n_mem = mem_k.shape[:3]
        mem_k = mem_k.reshape(depth, bsz, n_mem * heads, hd)
        mem_v = mem_v.reshape(depth, bsz, n_mem * heads, hd)
        mem_spec = pl.BlockSpec((None, None, n_mem * heads, hd), lambda b, i: (layer, b, 0, 0))
    else:
        n_mem = mem_k.shape[1]
        mem_spec = pl.BlockSpec((None, n_mem, w), lambda b, i: (b, 0, 0))
    return pl.pallas_call(
        functools.partial(_cross_kernel, heads=heads, hd=hd, by_head=by_head),
        grid=(bsz, n),
        in_specs=[pl.BlockSpec((tm, d), lambda b, i: (b * n + i, 0)), pl.BlockSpec((1, d), full),
                  pl.BlockSpec((d, w), full), pl.BlockSpec((1, hd), full),
                  mem_spec, mem_spec,
                  pl.BlockSpec((w, d), full)],
        out_specs=pl.BlockSpec((tm, d), lambda b, i: (b * n + i, 0)),
        out_shape=jax.ShapeDtypeStruct((t, d), F32),
        compiler_params=_params(2),
        name="cross_attn",
    )(x, gx, wq, gq, mem_k, mem_v, wo)


def _ffn_kernel(x_ref, g_ref, w1_ref, w2_ref, o_ref, h_scr, acc_scr):
    k = pl.program_id(1)

    @pl.when(k == 0)
    def _():
        h_scr[...] = _rms(x_ref[...], g_ref[...]).astype(BF16)
        acc_scr[...] = jnp.zeros_like(acc_scr)

    a = jnp.maximum(_dot(h_scr[...], w1_ref[...]), 0.0)
    acc_scr[...] += _dot((a * a).astype(BF16), w2_ref[...])

    @pl.when(k == pl.num_programs(1) - 1)
    def _():
        o_ref[...] = x_ref[...] + acc_scr[...]


def _ffn(x, g, w1, w2):
    t, d = x.shape
    dff = w1.shape[1]
    tm = _pick(t, (1024, 512, 256))
    tf = 1024
    return pl.pallas_call(
        _ffn_kernel,
        grid=(t // tm, dff // tf),
        in_specs=[pl.BlockSpec((tm, d), lambda i, k: (i, 0)), pl.BlockSpec((1, d), lambda i, k: (0, 0)),
                  pl.BlockSpec((d, tf), lambda i, k: (0, k)), pl.BlockSpec((tf, d), lambda i, k: (k, 0))],
        out_specs=pl.BlockSpec((tm, d), lambda i, k: (i, 0)),
        out_shape=jax.ShapeDtypeStruct((t, d), F32),
        scratch_shapes=[pltpu.VMEM((tm, d), BF16), pltpu.VMEM((tm, d), F32)],
        compiler_params=_params(2),
        name="ffn",
    )(x, g, w1, w2)


def _layer(x, bsz, seq, conv_hist, ret_state0, ret_tables, mem_k, mem_v, layer, lw, dims, fox):
    d_conv, ret_heads, ret_dk, ret_dv, fox_heads, fox_hd, x_heads, x_hd = dims
    prompt = fox[0] == "prompt"
    z_dtype = BF16 if seq % BF16_ROWS == 0 else F32
    z, lf = _in_proj(x, lw["g_mix"], lw["w_main"], lw["b_main"], lw["w_f"], lw["b_f"], z_dtype)
    c_act, new_buf = _conv_branch(z, conv_hist, lw["conv_w"], lw["conv_b"], lw["conv_ln_g"], lw["conv_ln_b"],
                                  bsz, seq)
    ret_col0 = 2 * d_conv
    r_act, new_state = _ret_branch(z, ret_tables, lw["ret_gn"], ret_state0, bsz, seq, ret_col0)
    fox_col0 = ret_col0 + 2 * ret_heads * ret_dk + 2 * ret_heads * ret_dv
    if prompt:
        _, kbuf, vbuf = fox
        qw, kw, vw, kbuf, vbuf = _fox_prep_prompt(z, lf, lw["fox_g_q"], lw["fox_g_k"], kbuf, vbuf, layer,
                                                  bsz, seq, fox_col0, fox_heads, fox_hd)
        fo = _fox_flash(qw, kw, vw, bsz, seq, fox_heads, fox_hd, FLASH_HEADS)
        fox_out = (kbuf, vbuf)
    else:
        _, kt_cache, vt_cache, lf_cache, page_table = fox
        qn, kf, kb, vf, vb, c = _fox_prep(z, lf, lw["fox_g_q"], lw["fox_g_k"], bsz, seq, fox_col0,
                                          fox_heads, fox_hd)
        fo = _fox_paged(qn, kb, vb, c, kt_cache, vt_cache, lf_cache, layer, page_table, seq)
        fox_out = (kf, vf)
    gate_col0 = fox_col0 + 3 * fox_heads * fox_hd
    x = _merge(x, z, c_act, r_act, fo, lw["conv_w_pw"], lw["conv_b_pw"], lw["ret_w_o"], lw["fox_w_o"],
               lw["w_out"], gate_col0)
    x = _cross(x, lw["g_x"], lw["x_w_q"], lw["x_g_q"], mem_k, mem_v, layer, lw["x_w_o"], bsz, seq, x_heads, x_hd)
    x = _ffn(x, lw["g_ff"], lw["w_ff1"], lw["w_ff2"])
    return x, new_buf, new_state, fox_out, lf[:, :fox_heads]


def kernel(x_prompt, x_sample, cache_conv, state_ret, cache_fox_k, cache_fox_v, cache_fox_logf, cache_mem_k, cache_mem_v, page_table, mem_prompt, g_mix, w_in, b_in, conv_w, conv_b, conv_ln_g, conv_ln_b, conv_w_pw, conv_b_pw, ret_gn, ret_w_o, fox_g_q, fox_g_k, fox_w_o, w_out, g_x, g_mem, x_w_q, x_w_k, x_w_v, x_g_q, x_g_k, x_w_o, g_ff, w_ff1, w_ff2):
    bp, sp, d = x_prompt.shape
    bs, ss, _ = x_sample.shape
    depth = w_in.shape[0]
    width, d_conv = conv_w.shape[1], conv_w.shape[2]
    ret_heads, ret_dk, ret_dv = state_ret.shape[2], state_ret.shape[3], state_ret.shape[4]
    n_pool, page, fox_heads, fox_hd = cache_fox_k.shape[1:]
    n_mem, x_heads, x_hd = cache_mem_k.shape[2:]
    n_pages = page_table.shape[1]
    past = n_pages * page
    dims = (d_conv, ret_heads, ret_dk, ret_dv, fox_heads, fox_hd, x_heads, x_hd)
    fw = fox_heads * fox_hd
    forget_col = 2 * d_conv + 2 * ret_heads * ret_dk + 2 * ret_heads * ret_dv + 3 * fw

    tables_p = _ret_tables(sp, 0, ret_heads, ret_dk, ret_dv)
    tables_s = _ret_tables(ss, past, ret_heads, ret_dk, ret_dv)
    conv0 = jnp.zeros((bp, width - 1, d_conv), F32)
    ret0 = jnp.zeros((bp, ret_heads, ret_dk, ret_dv), F32)
    kt_cache = jnp.transpose(cache_fox_k, (0, 1, 3, 4, 2))
    vt_cache = jnp.transpose(cache_fox_v, (0, 1, 3, 4, 2))
    lf_cache = jnp.transpose(cache_fox_logf, (0, 1, 3, 2))
    kbuf = jnp.zeros((depth, bp, fw, sp), F32)
    vbuf = jnp.zeros((depth, bp, fw, sp), F32)

    xp = x_prompt.reshape(bp * sp, d)
    xs = x_sample.reshape(bs * ss, d)
    mem2d = mem_prompt.reshape(bp * n_mem, d)
    outs_p = [[] for _ in range(5)]
    outs_s = [[] for _ in range(5)]
    row = lambda a: a.reshape(1, -1)
    for l in range(depth):
        w_f = jnp.pad(w_in[l][:, forget_col:forget_col + fox_heads], ((0, 0), (0, LANES - fox_heads)))
        b_f = jnp.pad(b_in[l][forget_col:forget_col + fox_heads], (0, LANES - fox_heads))
        lw = {
            "g_mix": row(g_mix[l]),
            "w_main": jnp.concatenate([w_in[l][:, :forget_col], w_in[l][:, forget_col + fox_heads:]],
                                      axis=1).astype(BF16),
            "b_main": row(jnp.concatenate([b_in[l][:forget_col], b_in[l][forget_col + fox_heads:]])),
            "w_f": w_f.astype(BF16), "b_f": row(b_f),
            "conv_w": conv_w[l], "conv_b": row(conv_b[l]),
            "conv_ln_g": row(conv_ln_g[l]), "conv_ln_b": row(conv_ln_b[l]),
            "conv_w_pw": conv_w_pw[l].astype(BF16), "conv_b_pw": row(conv_b_pw[l]),
            "ret_gn": row(ret_gn[l]), "ret_w_o": ret_w_o[l].astype(BF16),
            "fox_g_q": row(jnp.tile(fox_g_q[l], fox_heads)), "fox_g_k": row(jnp.tile(fox_g_k[l], fox_heads)),
            "fox_w_o": fox_w_o[l].astype(BF16), "w_out": w_out[l].astype(BF16),
            "g_x": row(g_x[l]), "x_w_q": x_w_q[l].astype(BF16), "x_g_q": row(x_g_q[l]),
            "x_w_o": x_w_o[l].astype(BF16), "g_ff": row(g_ff[l]),
            "w_ff1": w_ff1[l].astype(BF16), "w_ff2": w_ff2[l].astype(BF16),
        }
        mk, mv = _mem_kv(mem2d, row(g_mem[l]), x_w_k[l].astype(BF16), x_w_v[l].astype(BF16), row(x_g_k[l]),
                         x_heads, x_hd)
        mk3 = mk.reshape(bp, n_mem, x_heads * x_hd)
        mv3 = mv.reshape(bp, n_mem, x_heads * x_hd)
        xp, c_b, r_s, (kbuf, vbuf), l_f = _layer(xp, bp, sp, conv0, ret0, tables_p, mk3, mv3, l, lw, dims,
                                                 ("prompt", kbuf, vbuf))
        for lst, val in zip(outs_p, (c_b, r_s, l_f.reshape(bp, sp, fox_heads),
                                     mk.reshape(bp, n_mem, x_heads, x_hd), mv.reshape(bp, n_mem, x_heads, x_hd))):
            lst.append(val)
        xs, c_b, r_s, (f_k, f_v), l_f = _layer(xs, bs, ss, cache_conv[l], state_ret[l], tables_s,
                                               cache_mem_k, cache_mem_v, l, lw, dims,
                                               ("sample", kt_cache, vt_cache, lf_cache, page_table))
        for lst, val in zip(outs_s, (c_b, r_s, f_k.reshape(bs, ss, fox_heads, fox_hd),
                                     f_v.reshape(bs, ss, fox_heads, fox_hd), l_f.reshape(bs, ss, fox_heads))):
            lst.append(val)
    conv_p, ret_p, lf_p, mk_p, mv_p = [jnp.stack(o) for o in outs_p]
    fk_p = jnp.transpose(kbuf.reshape(depth, bp, fox_heads, fox_hd, sp), (0, 1, 4, 2, 3))
    fv_p = jnp.transpose(vbuf.reshape(depth, bp, fox_heads, fox_hd, sp), (0, 1, 4, 2, 3))
    return (xp.reshape(bp, sp, d), xs.reshape(bs, ss, d), conv_p, ret_p, fk_p, fv_p, lf_p, mk_p, mv_p,
            *[jnp.stack(o) for o in outs_s])
```
